```python
import math
import jax, jax.numpy as jnp
from jax import lax
import numpy as np

D_MODEL = 1024
BATCH = 4
SEQ = 8192
DEPTH = 2

CHUNK = 64
Q_BLOCK = 128
PLE_DIM = 256
MIX_WIDTH = D_MODEL // 2
POOL_WINDOWS = (2, 4, 8, 16)
POOL_GROUPS = len(POOL_WINDOWS)
POOL_GROUP_DIM = MIX_WIDTH // POOL_GROUPS
CONV_WIDTH = 3
ATTN_HEADS = 4
ATTN_HEAD_DIM = MIX_WIDTH // (2 * ATTN_HEADS)
ATTN_V_DIM = 2 * ATTN_HEAD_DIM
ATTN_QK_WIDTH = ATTN_HEADS * 2 * ATTN_HEAD_DIM
ATTN_V_WIDTH = ATTN_HEADS * ATTN_V_DIM
N_BRANCHES = 3
D_IN = 4 * MIX_WIDTH + 2 * ATTN_QK_WIDTH + ATTN_V_WIDTH + N_BRANCHES * D_MODEL
D_FF = 4 * D_MODEL
EPS = 1e-6
NEG_INF = -1e30

kernel_name = 'hybrid_pool_conv_diffattn_block'


def rms_norm(x, g):
    xf = x.astype(jnp.float32)
    y = xf * lax.rsqrt(jnp.mean(xf * xf, axis=-1, keepdims=True) + EPS)
    return (y * g.astype(jnp.float32)).astype(x.dtype)


def multiscale_pool(a, w, scale):
    b_, s_, _ = a.shape
    af = a.astype(jnp.float32).reshape(b_, s_, POOL_GROUPS, POOL_GROUP_DIM)
    cs = jnp.cumsum(af, axis=1)
    t = jnp.arange(s_)
    outs = []
    for g, win in enumerate(POOL_WINDOWS):
        csg = cs[:, :, g]
        prev = jnp.pad(csg, ((0, 0), (win, 0), (0, 0)))[:, :s_]
        cnt = jnp.minimum(t + 1, win).astype(jnp.float32)[None, :, None]
        outs.append((csg - prev) / cnt - af[:, :, g])
    pooled = jnp.stack(outs, axis=2)
    mixed = jnp.einsum('bsgc,gcd->bsgd', pooled, w.astype(jnp.float32))
    return (mixed.reshape(b_, s_, MIX_WIDTH) * scale.astype(jnp.float32)).astype(a.dtype)


def causal_depthwise_conv(z, w):
    c = z.shape[-1]
    return lax.conv_general_dilated(
        z, w[:, None, :].astype(z.dtype), window_strides=(1,),
        padding=[(CONV_WIDTH - 1, 0)], dimension_numbers=('NWC', 'WIO', 'NWC'),
        feature_group_count=c)


def diff_attention(q1, q2, k1, k2, v, lam):
    b_, s_, h_, d_ = q1.shape
    dv = v.shape[-1]
    nb = s_ // Q_BLOCK
    scale = d_ ** -0.5
    k1, k2, v = (t.transpose(0, 2, 1, 3) for t in (k1, k2, v))

    def to_blocks(q):
        return q.reshape(b_, nb, Q_BLOCK, h_, d_).transpose(1, 0, 3, 2, 4)

    key_chunk = jnp.arange(s_) // CHUNK

    def one_block(args):
        qb1, qb2, bi = args
        q_chunk = (bi * Q_BLOCK + jnp.arange(Q_BLOCK)) // CHUNK
        mask = key_chunk[None, :] <= q_chunk[:, None]

        def probs(q, k):
            s = jnp.einsum('bhqd,bhkd->bhqk', q, k).astype(jnp.float32) * scale
            return jax.nn.softmax(jnp.where(mask, s, NEG_INF), axis=-1)

        att = probs(qb1, k1) - lam * probs(qb2, k2)
        return jnp.einsum('bhqk,bhkd->bhqd', att.astype(v.dtype), v)

    out = lax.map(one_block, (to_blocks(q1), to_blocks(q2), jnp.arange(nb)))
    return out.transpose(1, 0, 3, 2, 4).reshape(b_, s_, h_, dv)


def setup_inputs(seed: int = 0) -> dict:
    key = jax.random.key(seed)
    ks = jax.random.split(key, 24)
    nrm = lambda k, shape, s: jax.random.normal(k, shape, jnp.float32) * s
    gain = lambda k, shape: 1.0 + 0.1 * jax.random.normal(k, shape, jnp.float32)
    L = DEPTH
    return {
        'x': nrm(ks[0], (BATCH, SEQ, D_MODEL), 1.0),
        'p': nrm(ks[1], (DEPTH, BATCH, SEQ, PLE_DIM), 1.0),
        'norm_mix_g': gain(ks[2], (L, D_MODEL)),
        'w_in': nrm(ks[3], (L, D_MODEL, D_IN), D_MODEL ** -0.5),
        'pool_w': nrm(ks[4], (L, POOL_GROUPS, POOL_GROUP_DIM, POOL_GROUP_DIM), POOL_GROUP_DIM ** -0.5),
        'pool_scale': gain(ks[5], (L, MIX_WIDTH)),
        'conv_w': nrm(ks[6], (L, CONV_WIDTH, MIX_WIDTH), CONV_WIDTH ** -0.5),
        'q_norm_g': gain(ks[7], (L, ATTN_HEAD_DIM)),
        'k_norm_g': gain(ks[8], (L, ATTN_HEAD_DIM)),
        'lam_q1': nrm(ks[9], (L, ATTN_HEAD_DIM), 0.1),
        'lam_k1': nrm(ks[10], (L, ATTN_HEAD_DIM), 0.1),
        'lam_q2': nrm(ks[11], (L, ATTN_HEAD_DIM), 0.1),
        'lam_k2': nrm(ks[12], (L, ATTN_HEAD_DIM), 0.1),
        'sub_norm_g': gain(ks[13], (L, ATTN_V_DIM)),
        'w_pool_out': nrm(ks[14], (L, MIX_WIDTH, D_MODEL), MIX_WIDTH ** -0.5),
        'w_conv_out': nrm(ks[15], (L, MIX_WIDTH, D_MODEL), MIX_WIDTH ** -0.5),
        'w_attn_out': nrm(ks[16], (L, ATTN_V_WIDTH, D_MODEL), ATTN_V_WIDTH ** -0.5),
        'w_o': nrm(ks[17], (L, D_MODEL, D_MODEL), D_MODEL ** -0.5),
        'norm_mlp_g': gain(ks[18], (L, D_MODEL)),
        'w_up': nrm(ks[19], (L, D_MODEL, D_FF), D_MODEL ** -0.5),
        'w_down': nrm(ks[20], (L, D_FF, D_MODEL), D_FF ** -0.5),
        'norm_ple_g': gain(ks[21], (L, D_MODEL)),
        'w_ple_gate': nrm(ks[22], (L, D_MODEL, D_MODEL), D_MODEL ** -0.5),
        'w_ple_proj': nrm(ks[23], (L, PLE_DIM, D_MODEL), PLE_DIM ** -0.5),
    }


def reference(x, p, norm_mix_g, w_in, pool_w, pool_scale, conv_w, q_norm_g, k_norm_g,
              lam_q1, lam_k1, lam_q2, lam_k2, sub_norm_g, w_pool_out, w_conv_out,
              w_attn_out, w_o, norm_mlp_g, w_up, w_down, norm_ple_g, w_ple_gate, w_ple_proj):
    b_, s_, _ = x.shape
    sizes = [MIX_WIDTH, MIX_WIDTH, MIX_WIDTH, MIX_WIDTH,
             ATTN_QK_WIDTH, ATTN_QK_WIDTH, ATTN_V_WIDTH]
    split_at = [int(c) for c in np.cumsum(sizes)]
    h = x
    for i in range(DEPTH):
        u = rms_norm(h, norm_mix_g[i])
        proj = u @ w_in[i]
        a_in, c_x, c_b, c_c, q, k, v, gates = jnp.split(proj, split_at, axis=-1)

        y_a = multiscale_pool(a_in, pool_w[i], pool_scale[i])

        y_b = c_b * causal_depthwise_conv(c_c * c_x, conv_w[i])

        q = q.reshape(b_, s_, ATTN_HEADS, 2, ATTN_HEAD_DIM)
        k = k.reshape(b_, s_, ATTN_HEADS, 2, ATTN_HEAD_DIM)
        v = v.reshape(b_, s_, ATTN_HEADS, ATTN_V_DIM)
        q1 = rms_norm(q[..., 0, :], q_norm_g[i])
        q2 = rms_norm(q[..., 1, :], q_norm_g[i])
        k1 = rms_norm(k[..., 0, :], k_norm_g[i])
        k2 = rms_norm(k[..., 1, :], k_norm_g[i])
        lam_init = 0.8 - 0.6 * math.exp(-0.3 * i)
        lam = (jnp.exp(jnp.sum(lam_q1[i].astype(jnp.float32) * lam_k1[i].astype(jnp.float32)))
               - jnp.exp(jnp.sum(lam_q2[i].astype(jnp.float32) * lam_k2[i].astype(jnp.float32)))
               + lam_init)
        o = diff_attention(q1, q2, k1, k2, v, lam)
        o = rms_norm(o, sub_norm_g[i]) * (1.0 - lam_init)
        y_c = o.reshape(b_, s_, ATTN_V_WIDTH)

        g_a, g_b, g_c = jnp.split(jax.nn.sigmoid(gates), N_BRANCHES, axis=-1)
        merged = (g_a * (y_a @ w_pool_out[i])
                  + g_b * (y_b @ w_conv_out[i])
                  + g_c * (y_c @ w_attn_out[i]))
        h = h + merged @ w_o[i]

        m = rms_norm(h, norm_mlp_g[i])
        h = h + jnp.square(jax.nn.relu(m @ w_up[i])) @ w_down[i]

        e = rms_norm(h, norm_ple_g[i])
        h = h + jax.nn.sigmoid(e @ w_ple_gate[i]) * (p[i] @ w_ple_proj[i])
    return h
```

```python
import functools
import math

import jax
import jax.numpy as jnp
from jax import lax
from jax.experimental import pallas as pl
from jax.experimental.pallas import tpu as pltpu

MIX_WIDTH = 512
POOL_WINDOWS = (2, 4, 8, 16)
POOL_GROUP_DIM = MIX_WIDTH // len(POOL_WINDOWS)
CONV_WIDTH = 3
ATTN_HEADS = 4
ATTN_HEAD_DIM = 64
ATTN_V_DIM = 2 * ATTN_HEAD_DIM
CHUNK = 64
EPS = 1e-6
NEG_INF = -1e30

POOL_HALO = 16
CONV_HALO = 8
VMEM_LIMIT_BYTES = 56 * 1024 * 1024

F32 = jnp.float32
BF16 = jnp.bfloat16


def _dot(a, b):
    return jnp.dot(a, b, preferred_element_type=F32)


def _dot_nt(a, b):
    return lax.dot_general(a, b, (((1,), (1,)), ((), ())), preferred_element_type=F32)


def _rms_rows(x, g):
    ms = jnp.mean(x * x, axis=-1, keepdims=True)
    return x * lax.rsqrt(ms + EPS) * g


def _resident(shape):
    nd = len(shape)
    return pl.BlockSpec(shape, lambda *_: (0,) * nd, pipeline_mode=pl.Buffered(1))


def _mixer_in_kernel(h_ref, gmix_ref, w_in_ref, w_vt_ref, seg_ref, gq_ref, gk_ref,
                     pool_w_ref, pool_scale_ref, conv_w_ref, w_pool_out_ref, w_conv_out_ref,
                     partial_ref, gc_ref, q_ref, k_ref, vt_ref,
                     abuf_ref, zbuf_ref, *, tm):
    s_idx = pl.program_id(1)
    c = MIX_WIDTH

    @pl.when(s_idx == 0)
    def _():
        abuf_ref[0:POOL_HALO, :] = jnp.zeros((POOL_HALO, c), F32)
        zbuf_ref[0:CONV_HALO, :] = jnp.zeros((CONV_HALO, c), F32)

    u = _rms_rows(h_ref[...], gmix_ref[...]).astype(BF16)

    def proj(lo, width):
        return _dot(u, w_in_ref[:, lo:lo + width])

    a = proj(0, c)
    abuf_ref[POOL_HALO:POOL_HALO + tm, :] = a
    t_pos = (s_idx * tm + lax.broadcasted_iota(jnp.int32, (tm, POOL_GROUP_DIM), 0) + 1).astype(F32)
    pooled = []
    for g, win in enumerate(POOL_WINDOWS):
        lo = g * POOL_GROUP_DIM
        wsum = a[:, lo:lo + POOL_GROUP_DIM]
        for back in range(1, win):
            wsum = wsum + abuf_ref[POOL_HALO - back:POOL_HALO - back + tm, lo:lo + POOL_GROUP_DIM]
        cnt = jnp.minimum(t_pos, float(win))
        pooled.append(wsum / cnt - a[:, lo:lo + POOL_GROUP_DIM])
    pooled = jnp.concatenate(pooled, axis=-1).astype(BF16)
    y_a = _dot(pooled, pool_w_ref[...]) * pool_scale_ref[...]
    abuf_ref[0:POOL_HALO, :] = abuf_ref[tm:tm + POOL_HALO, :]

    z = proj(3 * c, c) * proj(c, c)
    zbuf_ref[CONV_HALO:CONV_HALO + tm, :] = z
    conv = conv_w_ref[CONV_WIDTH - 1:CONV_WIDTH, :] * z
    for back in range(1, CONV_WIDTH):
        tap = conv_w_ref[CONV_WIDTH - 1 - back:CONV_WIDTH - back, :]
        conv = conv + tap * zbuf_ref[CONV_HALO - back:CONV_HALO - back + tm, :]
    y_b = proj(2 * c, c) * conv
    zbuf_ref[0:CONV_HALO, :] = zbuf_ref[tm:tm + CONV_HALO, :]

    gate_lo = 7 * c
    d = partial_ref.shape[-1]
    g_a = jax.nn.sigmoid(proj(gate_lo, d))
    part = g_a * _dot(y_a.astype(BF16), w_pool_out_ref[...])
    g_b = jax.nn.sigmoid(proj(gate_lo + d, d))
    part = part + g_b * _dot(y_b.astype(BF16), w_conv_out_ref[...])
    partial_ref[...] = part
    gc_ref[...] = jax.nn.sigmoid(proj(gate_lo + 2 * d, d))

    def qk_norm(x, g):
        ss = _dot((x * x).astype(BF16), seg_ref[...])
        return (x * lax.rsqrt(ss * (1.0 / ATTN_HEAD_DIM) + EPS) * g).astype(BF16)

    qn = qk_norm(proj(4 * c, c), gq_ref[...])
    kn = qk_norm(proj(5 * c, c), gk_ref[...])
    for hd in range(ATTN_HEADS):
        q_ref[hd] = qn[:, hd * ATTN_V_DIM:(hd + 1) * ATTN_V_DIM]
        k_ref[hd] = kn[:, hd * ATTN_V_DIM:(hd + 1) * ATTN_V_DIM]
    vt_ref[...] = _dot_nt(w_vt_ref[...], u).astype(BF16)


def _mixer_in(h, gmix, w_in, w_vt, seg, gq, gk, pool_w, pool_scale, conv_w, w_pool_out,
              w_conv_out, *, tm):
    b, s, d = h.shape
    c = MIX_WIDTH
    tok = lambda width: pl.BlockSpec((None, tm, width), lambda bi, si: (bi, si, 0))
    head_tok = pl.BlockSpec((None, ATTN_HEADS, tm, ATTN_V_DIM), lambda bi, si: (bi, 0, si, 0))
    return pl.pallas_call(
        functools.partial(_mixer_in_kernel, tm=tm),
        grid=(b, s // tm),
        in_specs=[tok(d), _resident(gmix.shape), _resident(w_in.shape), _resident(w_vt.shape),
                  _resident(seg.shape), _resident(gq.shape), _resident(gk.shape),
                  _resident(pool_w.shape), _resident(pool_scale.shape), _resident(conv_w.shape),
                  _resident(w_pool_out.shape), _resident(w_conv_out.shape)],
        out_specs=[tok(d), tok(d), head_tok, head_tok,
                   pl.BlockSpec((None, c, tm), lambda bi, si: (bi, 0, si))],
        out_shape=[jax.ShapeDtypeStruct((b, s, d), F32),
                   jax.ShapeDtypeStruct((b, s, d), F32),
                   jax.ShapeDtypeStruct((b, ATTN_HEADS, s, ATTN_V_DIM), BF16),
                   jax.ShapeDtypeStruct((b, ATTN_HEADS, s, ATTN_V_DIM), BF16),
                   jax.ShapeDtypeStruct((b, c, s), BF16)],
        scratch_shapes=[pltpu.VMEM((POOL_HALO + tm, c), F32),
                        pltpu.VMEM((CONV_HALO + tm, c), F32)],
        compiler_params=pltpu.CompilerParams(
            dimension_semantics=("arbitrary", "arbitrary"),
            vmem_limit_bytes=VMEM_LIMIT_BYTES),
        name="mixer_in",
    )(h, gmix, w_in, w_vt, seg, gq, gk, pool_w, pool_scale, conv_w, w_pool_out, w_conv_out)


def _diff_attn_kernel(q_ref, k_ref, vt_ref, lam_ref, gsub_ref, o_ref,
                      acc1_ref, acc2_ref, *, tq, lam_init):
    qi = pl.program_id(2)
    q = q_ref[...]
    lane = lax.broadcasted_iota(jnp.int32, q.shape, 1)
    zero = jnp.zeros_like(q)
    q1 = jnp.where(lane < ATTN_HEAD_DIM, q, zero)
    q2 = jnp.where(lane >= ATTN_HEAD_DIM, q, zero)

    acc1_ref[...] = jnp.zeros_like(acc1_ref)
    acc2_ref[...] = jnp.zeros_like(acc2_ref)

    def update(s, m, l, acc_ref, vt):
        m_new = jnp.maximum(m, jnp.max(s, axis=0, keepdims=True))
        alpha = jnp.exp(m - m_new)
        p = jnp.exp(s - m_new)
        l_new = alpha * l + jnp.sum(p, axis=0, keepdims=True)
        acc_ref[...] = alpha * acc_ref[...] + _dot(vt, p.astype(BF16))
        return m_new, l_new

    def tile(j, carry, masked):
        m1, l1, m2, l2 = carry
        start = pl.multiple_of(j * tq, tq)
        k = k_ref[pl.ds(start, tq), :]
        vt = vt_ref[:, pl.ds(start, tq)]
        s1 = _dot_nt(k, q1)
        s2 = _dot_nt(k, q2)
        if masked:
            key_chunk = lax.broadcasted_iota(jnp.int32, s1.shape, 0) // CHUNK
            q_chunk = lax.broadcasted_iota(jnp.int32, s1.shape, 1) // CHUNK
            keep = key_chunk <= q_chunk
            s1 = jnp.where(keep, s1, NEG_INF)
            s2 = jnp.where(keep, s2, NEG_INF)
        m1, l1 = update(s1, m1, l1, acc1_ref, vt)
        m2, l2 = update(s2, m2, l2, acc2_ref, vt)
        return m1, l1, m2, l2

    init_m = jnp.full((1, tq), NEG_INF, F32)
    init_l = jnp.zeros((1, tq), F32)
    carry = lax.fori_loop(0, qi, lambda j, cr: tile(j, cr, False),
                          (init_m, init_l, init_m, init_l))
    _, l1, _, l2 = tile(qi, carry, True)

    lam = (jnp.exp(jnp.sum(lam_ref[0:1, :] * lam_ref[1:2, :], axis=-1, keepdims=True))
           - jnp.exp(jnp.sum(lam_ref[2:3, :] * lam_ref[3:4, :], axis=-1, keepdims=True))
           + lam_init)
    o = acc1_ref[...] / l1 - lam * (acc2_ref[...] / l2)
    ms = jnp.mean(o * o, axis=0, keepdims=True)
    o = o * lax.rsqrt(ms + EPS)
    o_ref[...] = (o.T * gsub_ref[...] * (1.0 - lam_init)).astype(o_ref.dtype)


def _diff_attn(q, k, vt, lam_vecs, gsub, *, tq, lam_init):
    b, nh, s, dv = q.shape
    return pl.pallas_call(
        functools.partial(_diff_attn_kernel, tq=tq, lam_init=lam_init),
        grid=(b, nh, s // tq),
        in_specs=[pl.BlockSpec((None, None, tq, dv), lambda bi, hi, qi: (bi, hi, qi, 0)),
                  pl.BlockSpec((None, None, s, dv), lambda bi, hi, qi: (bi, hi, 0, 0)),
                  pl.BlockSpec((None, dv, s), lambda bi, hi, qi: (bi, hi, 0)),
                  _resident(lam_vecs.shape), _resident(gsub.shape)],
        out_specs=pl.BlockSpec((None, tq, dv), lambda bi, hi, qi: (bi, qi, hi)),
        out_shape=jax.ShapeDtypeStruct((b, s, nh * dv), BF16),
        scratch_shapes=[pltpu.VMEM((dv, tq), F32), pltpu.VMEM((dv, tq), F32)],
        compiler_params=pltpu.CompilerParams(
            dimension_semantics=("arbitrary", "arbitrary", "arbitrary"),
            vmem_limit_bytes=VMEM_LIMIT_BYTES),
        name="diff_attn",
    )(q, k, vt, lam_vecs, gsub)


def _mixer_out_kernel(h_ref, partial_ref, gc_ref, yc_ref, p_ref, w_attn_out_ref, w_o_ref,
                      gmlp_ref, w_up_ref, w_down_ref, gple_ref, w_gate_ref, w_proj_ref,
                      out_ref, *, ff_chunk):
    merged = partial_ref[...] + gc_ref[...] * _dot(yc_ref[...], w_attn_out_ref[...])
    h1 = h_ref[...] + _dot(merged.astype(BF16), w_o_ref[...])

    m = _rms_rows(h1, gmlp_ref[...]).astype(BF16)
    d_ff = w_up_ref.shape[1]
    mlp = jnp.zeros_like(h1)
    for lo in range(0, d_ff, ff_chunk):
        up = _dot(m, w_up_ref[:, lo:lo + ff_chunk])
        act = jnp.square(jnp.maximum(up, 0.0)).astype(BF16)
        mlp = mlp + _dot(act, w_down_ref[lo:lo + ff_chunk, :])
    h2 = h1 + mlp

    e = _rms_rows(h2, gple_ref[...]).astype(BF16)
    gate = jax.nn.sigmoid(_dot(e, w_gate_ref[...]))
    out_ref[...] = h2 + gate * _dot(p_ref[...].astype(BF16), w_proj_ref[...])


def _mixer_out(h, partial, gc, yc, p, w_attn_out, w_o, gmlp, w_up, w_down, gple, w_gate, w_proj,
               *, tm, ff_chunk):
    b, s, d = h.shape
    tok = lambda width: pl.BlockSpec((None, tm, width), lambda bi, si: (bi, si, 0))
    return pl.pallas_call(
        functools.partial(_mixer_out_kernel, ff_chunk=ff_chunk),
        grid=(b, s // tm),
        in_specs=[tok(d), tok(d), tok(d), tok(yc.shape[-1]), tok(p.shape[-1]),
                  _resident(w_attn_out.shape), _resident(w_o.shape), _resident(gmlp.shape),
                  _resident(w_up.shape), _resident(w_down.shape), _resident(gple.shape),
                  _resident(w_gate.shape), _resident(w_proj.shape)],
        out_specs=tok(d),
        out_shape=jax.ShapeDtypeStruct((b, s, d), F32),
        compiler_params=pltpu.CompilerParams(
            dimension_semantics=("arbitrary", "arbitrary"),
            vmem_limit_bytes=VMEM_LIMIT_BYTES),
        name="mixer_out",
    )(h, partial, gc, yc, p, w_attn_out, w_o, gmlp, w_up, w_down, gple, w_gate, w_proj)


def _block_diag(blocks):
    g, n, _ = blocks.shape
    eye = jnp.eye(g, dtype=blocks.dtype)
    return (eye[:, None, :, None] * blocks[:, :, None, :]).reshape(g * n, g * n)


def kernel(x, p, norm_mix_g, w_in, pool_w, pool_scale, conv_w, q_norm_g, k_norm_g, lam_q1, lam_k1,
           lam_q2, lam_k2, sub_norm_g, w_pool_out, w_conv_out, w_attn_out, w_o, norm_mlp_g, w_up,
           w_down, norm_ple_g, w_ple_gate, w_ple_proj):
    depth = w_in.shape[0]
    c = MIX_WIDTH
    tm_in, tm_out, tq, ff_chunk = 256, 256, 256, 1024
    n_seg = c // ATTN_HEAD_DIM
    seg = _block_diag(jnp.ones((n_seg, ATTN_HEAD_DIM, ATTN_HEAD_DIM), BF16))
    row = lambda v: v.reshape(1, -1).astype(F32)
    score_scale = ATTN_HEAD_DIM ** -0.5

    h = x
    for i in range(depth):
        lam_init = 0.8 - 0.6 * math.exp(-0.3 * i)
        w_in_i = w_in[i]
        partial, gc, q, k, vt = _mixer_in(
            h, row(norm_mix_g[i]), w_in_i.astype(BF16),
            w_in_i[:, 6 * c:7 * c].T.astype(BF16), seg,
            row(jnp.tile(q_norm_g[i], n_seg) * score_scale), row(jnp.tile(k_norm_g[i], n_seg)),
            _block_diag(pool_w[i]).astype(BF16), row(pool_scale[i]), conv_w[i].astype(F32),
            w_pool_out[i].astype(BF16), w_conv_out[i].astype(BF16), tm=tm_in)
        lam_vecs = jnp.stack([lam_q1[i], lam_k1[i], lam_q2[i], lam_k2[i]]).astype(F32)
        yc = _diff_attn(q, k, vt, lam_vecs, row(sub_norm_g[i]), tq=tq, lam_init=lam_init)
        h = _mixer_out(
            h, partial, gc, yc, p[i], w_attn_out[i].astype(BF16), w_o[i].astype(BF16),
            row(norm_mlp_g[i]), w_up[i].astype(BF16), w_down[i].astype(BF16),
            row(norm_ple_g[i]), w_ple_gate[i].astype(BF16), w_ple_proj[i].astype(BF16),
            tm=tm_out, ff_chunk=ff_chunk)
    return h
```

```python
import functools
import math

import jax
import jax.numpy as jnp
from jax import lax
from jax.experimental import pallas as pl
from jax.experimental.pallas import tpu as pltpu

MIX_WIDTH = 512
POOL_WINDOWS = (2, 4, 8, 16)
POOL_GROUP_DIM = MIX_WIDTH // len(POOL_WINDOWS)
CONV_WIDTH = 3
ATTN_HEADS = 4
ATTN_HEAD_DIM = 64
ATTN_V_DIM = 2 * ATTN_HEAD_DIM
CHUNK = 64
EPS = 1e-6
NEG_INF = -1e30

POOL_HALO = 16
CONV_HALO = 8
VMEM_LIMIT_BYTES = 56 * 1024 * 1024

F32 = jnp.float32
BF16 = jnp.bfloat16


def _dot(a, b):
    return jnp.dot(a, b, preferred_element_type=F32)


def _dot_nt(a, b):
    return lax.dot_general(a, b, (((1,), (1,)), ((), ())), preferred_element_type=F32)


def _rms_rows(x, g):
    ms = jnp.mean(x * x, axis=-1, keepdims=True)
    return x * lax.rsqrt(ms + EPS) * g


def _resident(shape):
    nd = len(shape)
    return pl.BlockSpec(shape, lambda *_: (0,) * nd, pipeline_mode=pl.Buffered(1))


def _mixer_in_kernel(h_ref, gmix_ref, w_in_ref, w_vt_ref, seg_ref, gq_ref, gk_ref,
                     pool_w_ref, pool_scale_ref, conv_w_ref, w_pool_out_ref, w_conv_out_ref,
                     partial_ref, gc_ref, q_ref, k_ref, vt_ref,
                     abuf_ref, zbuf_ref, *, tm):
    s_idx = pl.program_id(1)
    c = MIX_WIDTH

    @pl.when(s_idx == 0)
    def _():
        abuf_ref[0:POOL_HALO, :] = jnp.zeros((POOL_HALO, c), F32)
        zbuf_ref[0:CONV_HALO, :] = jnp.zeros((CONV_HALO, c), F32)

    u = _rms_rows(h_ref[...], gmix_ref[...]).astype(BF16)

    def proj(lo, width):
        return _dot(u, w_in_ref[:, lo:lo + width])

    a = proj(0, c)
    abuf_ref[POOL_HALO:POOL_HALO + tm, :] = a
    t_pos = (s_idx * tm + lax.broadcasted_iota(jnp.int32, (tm, POOL_GROUP_DIM), 0) + 1).astype(F32)
    pooled = []
    for g, win in enumerate(POOL_WINDOWS):
        lo = g * POOL_GROUP_DIM
        wsum = a[:, lo:lo + POOL_GROUP_DIM]
        for back in range(1, win):
            wsum = wsum + abuf_ref[POOL_HALO - back:POOL_HALO - back + tm, lo:lo + POOL_GROUP_DIM]
        cnt = jnp.minimum(t_pos, float(win))
        pooled.append(wsum / cnt - a[:, lo:lo + POOL_GROUP_DIM])
    pooled = jnp.concatenate(pooled, axis=-1).astype(BF16)
    y_a = _dot(pooled, pool_w_ref[...]) * pool_scale_ref[...]
    abuf_ref[0:POOL_HALO, :] = abuf_ref[tm:tm + POOL_HALO, :]

    z = proj(3 * c, c) * proj(c, c)
    zbuf_ref[CONV_HALO:CONV_HALO + tm, :] = z
    conv = conv_w_ref[CONV_WIDTH - 1:CONV_WIDTH, :] * z
    for back in range(1, CONV_WIDTH):
        tap = conv_w_ref[CONV_WIDTH - 1 - back:CONV_WIDTH - back, :]
        conv = conv + tap * zbuf_ref[CONV_HALO - back:CONV_HALO - back + tm, :]
    y_b = proj(2 * c, c) * conv
    zbuf_ref[0:CONV_HALO, :] = zbuf_ref[tm:tm + CONV_HALO, :]

    gate_lo = 7 * c
    d = partial_ref.shape[-1]
    g_a = jax.nn.sigmoid(proj(gate_lo, d))
    part = g_a * _dot(y_a.astype(BF16), w_pool_out_ref[...])
    g_b = jax.nn.sigmoid(proj(gate_lo + d, d))
    part = part + g_b * _dot(y_b.astype(BF16), w_conv_out_ref[...])
    partial_ref[...] = part
    gc_ref[...] = jax.nn.sigmoid(proj(gate_lo + 2 * d, d))

    def qk_norm(x, g):
        ss = _dot((x * x).astype(BF16), seg_ref[...])
        return (x * lax.rsqrt(ss * (1.0 / ATTN_HEAD_DIM) + EPS) * g).astype(BF16)

    qn = qk_norm(proj(4 * c, c), gq_ref[...])
    kn = qk_norm(proj(5 * c, c), gk_ref[...])
    for hd in range(ATTN_HEADS):
        q_ref[hd] = qn[:, hd * ATTN_V_DIM:(hd + 1) * ATTN_V_DIM]
        k_ref[hd] = kn[:, hd * ATTN_V_DIM:(hd + 1) * ATTN_V_DIM]
    vt_ref[...] = _dot_nt(w_vt_ref[...], u).astype(BF16)


def _mixer_in(h, gmix, w_in, w_vt, seg, gq, gk, pool_w, pool_scale, conv_w, w_pool_out,
              w_conv_out, *, tm):
    b, s, d = h.shape
    c = MIX_WIDTH
    tok = lambda width: pl.BlockSpec((None, tm, width), lambda bi, si: (bi, si, 0))
    head_tok = pl.BlockSpec((None, ATTN_HEADS, tm, ATTN_V_DIM), lambda bi, si: (bi, 0, si, 0))
    return pl.pallas_call(
        functools.partial(_mixer_in_kernel, tm=tm),
        grid=(b, s // tm),
        in_specs=[tok(d), _resident(gmix.shape), _resident(w_in.shape), _resident(w_vt.shape),
                  _resident(seg.shape), _resident(gq.shape), _resident(gk.shape),
                  _resident(pool_w.shape), _resident(pool_scale.shape), _resident(conv_w.shape),
                  _resident(w_pool_out.shape), _resident(w_conv_out.shape)],
        out_specs=[tok(d), tok(d), head_tok, head_tok,
                   pl.BlockSpec((None, c, tm), lambda bi, si: (bi, 0, si))],
        out_shape=[jax.ShapeDtypeStruct((b, s, d), F32),
                   jax.ShapeDtypeStruct((b, s, d), F32),
                   jax.ShapeDtypeStruct((b, ATTN_HEADS, s, ATTN_V_DIM), BF16),
                   jax.ShapeDtypeStruct((b, ATTN_HEADS, s, ATTN_V_DIM), BF16),
                   jax.ShapeDtypeStruct((b, c, s), BF16)],
        scratch_shapes=[pltpu.VMEM((POOL_HALO + tm, c), F32),
                        pltpu.VMEM((CONV_HALO + tm, c), F32)],
        compiler_params=pltpu.CompilerParams(
            dimension_semantics=("arbitrary", "arbitrary"),
            vmem_limit_bytes=VMEM_LIMIT_BYTES),
        name="mixer_in",
    )(h, gmix, w_in, w_vt, seg, gq, gk, pool_w, pool_scale, conv_w, w_pool_out, w_conv_out)


def _diff_attn_kernel(q_ref, k_ref, vt_ref, lam_ref, gsub_ref, o_ref,
                      acc1_ref, acc2_ref, *, tq, lam_init):
    qi = pl.program_id(2)
    q = q_ref[...]
    lane = lax.broadcasted_iota(jnp.int32, q.shape, 1)
    zero = jnp.zeros_like(q)
    q1 = jnp.where(lane < ATTN_HEAD_DIM, q, zero)
    q2 = jnp.where(lane >= ATTN_HEAD_DIM, q, zero)

    acc1_ref[...] = jnp.zeros_like(acc1_ref)
    acc2_ref[...] = jnp.zeros_like(acc2_ref)

    def update(s, m, l, acc_ref, vt):
        m_new = jnp.maximum(m, jnp.max(s, axis=0, keepdims=True))
        alpha = jnp.exp2(m - m_new)
        p = jnp.exp2(s - m_new)
        l_new = alpha * l + jnp.sum(p, axis=0, keepdims=True)
        acc_ref[...] = alpha * acc_ref[...] + _dot(vt, p.astype(BF16))
        return m_new, l_new

    def tile(j, carry, masked):
        m1, l1, m2, l2 = carry
        start = pl.multiple_of(j * tq, tq)
        k = k_ref[pl.ds(start, tq), :]
        vt = vt_ref[:, pl.ds(start, tq)]
        s1 = _dot_nt(k, q1)
        s2 = _dot_nt(k, q2)
        if masked:
            key_chunk = lax.broadcasted_iota(jnp.int32, s1.shape, 0) // CHUNK
            q_chunk = lax.broadcasted_iota(jnp.int32, s1.shape, 1) // CHUNK
            keep = key_chunk <= q_chunk
            s1 = jnp.where(keep, s1, NEG_INF)
            s2 = jnp.where(keep, s2, NEG_INF)
        m1, l1 = update(s1, m1, l1, acc1_ref, vt)
        m2, l2 = update(s2, m2, l2, acc2_ref, vt)
        return m1, l1, m2, l2

    init_m = jnp.full((1, tq), NEG_INF, F32)
    init_l = jnp.zeros((1, tq), F32)
    carry = lax.fori_loop(0, qi, lambda j, cr: tile(j, cr, False),
                          (init_m, init_l, init_m, init_l))
    _, l1, _, l2 = tile(qi, carry, True)

    lam = (jnp.exp(jnp.sum(lam_ref[0:1, :] * lam_ref[1:2, :], axis=-1, keepdims=True))
           - jnp.exp(jnp.sum(lam_ref[2:3, :] * lam_ref[3:4, :], axis=-1, keepdims=True))
           + lam_init)
    o = acc1_ref[...] / l1 - lam * (acc2_ref[...] / l2)
    ms = jnp.mean(o * o, axis=0, keepdims=True)
    o = o * lax.rsqrt(ms + EPS)
    o_ref[...] = (o.T * gsub_ref[...] * (1.0 - lam_init)).astype(o_ref.dtype)


def _diff_attn(q, k, vt, lam_vecs, gsub, *, tq, lam_init):
    b, nh, s, dv = q.shape
    return pl.pallas_call(
        functools.partial(_diff_attn_kernel, tq=tq, lam_init=lam_init),
        grid=(b, nh, s // tq),
        in_specs=[pl.BlockSpec((None, None, tq, dv), lambda bi, hi, qi: (bi, hi, qi, 0)),
                  pl.BlockSpec((None, None, s, dv), lambda bi, hi, qi: (bi, hi, 0, 0)),
                  pl.BlockSpec((None, dv, s), lambda bi, hi, qi: (bi, hi, 0)),
                  _resident(lam_vecs.shape), _resident(gsub.shape)],
        out_specs=pl.BlockSpec((None, tq, dv), lambda bi, hi, qi: (bi, qi, hi)),
        out_shape=jax.ShapeDtypeStruct((b, s, nh * dv), BF16),
        scratch_shapes=[pltpu.VMEM((dv, tq), F32), pltpu.VMEM((dv, tq), F32)],
        compiler_params=pltpu.CompilerParams(
            dimension_semantics=("arbitrary", "arbitrary", "arbitrary"),
            vmem_limit_bytes=VMEM_LIMIT_BYTES),
        name="diff_attn",
    )(q, k, vt, lam_vecs, gsub)


def _mixer_out_kernel(h_ref, partial_ref, gc_ref, yc_ref, p_ref, w_attn_out_ref, w_o_ref,
                      gmlp_ref, w_up_ref, w_down_ref, gple_ref, w_gate_ref, w_proj_ref,
                      out_ref, *, ff_chunk):
    merged = partial_ref[...] + gc_ref[...] * _dot(yc_ref[...], w_attn_out_ref[...])
    h1 = h_ref[...] + _dot(merged.astype(BF16), w_o_ref[...])

    m = _rms_rows(h1, gmlp_ref[...]).astype(BF16)
    d_ff = w_up_ref.shape[1]
    mlp = jnp.zeros_like(h1)
    for lo in range(0, d_ff, ff_chunk):
        up = _dot(m, w_up_ref[:, lo:lo + ff_chunk])
        act = jnp.square(jnp.maximum(up, 0.0)).astype(BF16)
        mlp = mlp + _dot(act, w_down_ref[lo:lo + ff_chunk, :])
    h2 = h1 + mlp

    e = _rms_rows(h2, gple_ref[...]).astype(BF16)
    gate = jax.nn.sigmoid(_dot(e, w_gate_ref[...]))
    out_ref[...] = h2 + gate * _dot(p_ref[...].astype(BF16), w_proj_ref[...])


def _mixer_out(h, partial, gc, yc, p, w_attn_out, w_o, gmlp, w_up, w_down, gple, w_gate, w_proj,
               *, tm, ff_chunk):
    b, s, d = h.shape
    tok = lambda width: pl.BlockSpec((None, tm, width), lambda bi, si: (bi, si, 0))
    return pl.pallas_call(
        functools.partial(_mixer_out_kernel, ff_chunk=ff_chunk),
        grid=(b, s // tm),
        in_specs=[tok(d), tok(d), tok(d), tok(yc.shape[-1]), tok(p.shape[-1]),
                  _resident(w_attn_out.shape), _resident(w_o.shape), _resident(gmlp.shape),
                  _resident(w_up.shape), _resident(w_down.shape), _resident(gple.shape),
                  _resident(w_gate.shape), _resident(w_proj.shape)],
        out_specs=tok(d),
        out_shape=jax.ShapeDtypeStruct((b, s, d), F32),
        compiler_params=pltpu.CompilerParams(
            dimension_semantics=("arbitrary", "arbitrary"),
            vmem_limit_bytes=VMEM_LIMIT_BYTES),
        name="mixer_out",
    )(h, partial, gc, yc, p, w_attn_out, w_o, gmlp, w_up, w_down, gple, w_gate, w_proj)


def _block_diag(blocks):
    g, n, _ = blocks.shape
    eye = jnp.eye(g, dtype=blocks.dtype)
    return (eye[:, None, :, None] * blocks[:, :, None, :]).reshape(g * n, g * n)


def kernel(x, p, norm_mix_g, w_in, pool_w, pool_scale, conv_w, q_norm_g, k_norm_g, lam_q1, lam_k1,
           lam_q2, lam_k2, sub_norm_g, w_pool_out, w_conv_out, w_attn_out, w_o, norm_mlp_g, w_up,
           w_down, norm_ple_g, w_ple_gate, w_ple_proj):
    depth = w_in.shape[0]
    c = MIX_WIDTH
    tm_in, tm_out, tq, ff_chunk = 256, 256, 512, 1024
    n_seg = c // ATTN_HEAD_DIM
    seg = _block_diag(jnp.ones((n_seg, ATTN_HEAD_DIM, ATTN_HEAD_DIM), BF16))
    row = lambda v: v.reshape(1, -1).astype(F32)
    score_scale = ATTN_HEAD_DIM ** -0.5 * math.log2(math.e)

    h = x
    for i in range(depth):
        lam_init = 0.8 - 0.6 * math.exp(-0.3 * i)
        w_in_i = w_in[i]
        partial, gc, q, k, vt = _mixer_in(
            h, row(norm_mix_g[i]), w_in_i.astype(BF16),
            w_in_i[:, 6 * c:7 * c].T.astype(BF16), seg,
            row(jnp.tile(q_norm_g[i], n_seg) * score_scale), row(jnp.tile(k_norm_g[i], n_seg)),
            _block_diag(pool_w[i]).astype(BF16), row(pool_scale[i]), conv_w[i].astype(F32),
            w_pool_out[i].astype(BF16), w_conv_out[i].astype(BF16), tm=tm_in)
        lam_vecs = jnp.stack([lam_q1[i], lam_k1[i], lam_q2[i], lam_k2[i]]).astype(F32)
        yc = _diff_attn(q, k, vt, lam_vecs, row(sub_norm_g[i]), tq=tq, lam_init=lam_init)
        h = _mixer_out(
            h, partial, gc, yc, p[i], w_attn_out[i].astype(BF16), w_o[i].astype(BF16),
            row(norm_mlp_g[i]), w_up[i].astype(BF16), w_down[i].astype(BF16),
            row(norm_ple_g[i]), w_ple_gate[i].astype(BF16), w_ple_proj[i].astype(BF16),
            tm=tm_out, ff_chunk=ff_chunk)
    return h
```

```python
import functools
import math

import jax
import jax.numpy as jnp
from jax import lax
from jax.experimental import pallas as pl
from jax.experimental.pallas import tpu as pltpu

MIX_WIDTH = 512
POOL_WINDOWS = (2, 4, 8, 16)
POOL_GROUP_DIM = MIX_WIDTH // len(POOL_WINDOWS)
CONV_WIDTH = 3
ATTN_HEADS = 4
ATTN_HEAD_DIM = 64
ATTN_V_DIM = 2 * ATTN_HEAD_DIM
CHUNK = 64
QUERY_GROUP = 256
MAX_STATIC_SHIFT = 30.0
SCORE_BOUND_MARGIN = 1.02
EPS = 1e-6
NEG_INF = -1e30

POOL_HALO = 16
CONV_HALO = 8
VMEM_LIMIT_BYTES = 56 * 1024 * 1024

F32 = jnp.float32
BF16 = jnp.bfloat16


def _dot(a, b):
    return jnp.dot(a, b, preferred_element_type=F32)


def _dot_nt(a, b):
    return lax.dot_general(a, b, (((1,), (1,)), ((), ())), preferred_element_type=F32)


def _rms_rows(x, g):
    ms = jnp.mean(x * x, axis=-1, keepdims=True)
    return x * lax.rsqrt(ms + EPS) * g


def _resident(shape):
    nd = len(shape)
    return pl.BlockSpec(shape, lambda *_: (0,) * nd, pipeline_mode=pl.Buffered(1))


def _mixer_in_kernel(h_ref, gmix_ref, w_in_ref, w_vt_ref, seg_ref, gq_ref, gk_ref,
                     pool_w_ref, pool_scale_ref, conv_w_ref, w_pool_out_ref, w_conv_out_ref,
                     partial_ref, gc_ref, q_ref, k_ref, vt_ref,
                     abuf_ref, zbuf_ref, *, tm):
    s_idx = pl.program_id(1)
    c = MIX_WIDTH

    @pl.when(s_idx == 0)
    def _():
        abuf_ref[0:POOL_HALO, :] = jnp.zeros((POOL_HALO, c), F32)
        zbuf_ref[0:CONV_HALO, :] = jnp.zeros((CONV_HALO, c), F32)

    u = _rms_rows(h_ref[...], gmix_ref[...]).astype(BF16)

    def proj(lo, width):
        return _dot(u, w_in_ref[:, lo:lo + width])

    a = proj(0, c)
    abuf_ref[POOL_HALO:POOL_HALO + tm, :] = a
    t_pos = (s_idx * tm + lax.broadcasted_iota(jnp.int32, (tm, POOL_GROUP_DIM), 0) + 1).astype(F32)
    pooled = []
    for g, win in enumerate(POOL_WINDOWS):
        lo = g * POOL_GROUP_DIM
        wsum = a[:, lo:lo + POOL_GROUP_DIM]
        for back in range(1, win):
            wsum = wsum + abuf_ref[POOL_HALO - back:POOL_HALO - back + tm, lo:lo + POOL_GROUP_DIM]
        cnt = jnp.minimum(t_pos, float(win))
        pooled.append(wsum / cnt - a[:, lo:lo + POOL_GROUP_DIM])
    pooled = jnp.concatenate(pooled, axis=-1).astype(BF16)
    y_a = _dot(pooled, pool_w_ref[...]) * pool_scale_ref[...]
    abuf_ref[0:POOL_HALO, :] = abuf_ref[tm:tm + POOL_HALO, :]

    z = proj(3 * c, c) * proj(c, c)
    zbuf_ref[CONV_HALO:CONV_HALO + tm, :] = z
    conv = conv_w_ref[CONV_WIDTH - 1:CONV_WIDTH, :] * z
    for back in range(1, CONV_WIDTH):
        tap = conv_w_ref[CONV_WIDTH - 1 - back:CONV_WIDTH - back, :]
        conv = conv + tap * zbuf_ref[CONV_HALO - back:CONV_HALO - back + tm, :]
    y_b = proj(2 * c, c) * conv
    zbuf_ref[0:CONV_HALO, :] = zbuf_ref[tm:tm + CONV_HALO, :]

    gate_lo = 7 * c
    d = partial_ref.shape[-1]
    g_a = jax.nn.sigmoid(proj(gate_lo, d))
    part = g_a * _dot(y_a.astype(BF16), w_pool_out_ref[...])
    g_b = jax.nn.sigmoid(proj(gate_lo + d, d))
    part = part + g_b * _dot(y_b.astype(BF16), w_conv_out_ref[...])
    partial_ref[...] = part
    gc_ref[...] = jax.nn.sigmoid(proj(gate_lo + 2 * d, d))

    def qk_norm(x, g):
        ss = _dot((x * x).astype(BF16), seg_ref[...])
        return (x * lax.rsqrt(ss * (1.0 / ATTN_HEAD_DIM) + EPS) * g).astype(BF16)

    qn = qk_norm(proj(4 * c, c), gq_ref[...])
    kn = qk_norm(proj(5 * c, c), gk_ref[...])
    for hd in range(ATTN_HEADS):
        q_ref[hd] = qn[:, hd * ATTN_V_DIM:(hd + 1) * ATTN_V_DIM]
        k_ref[hd] = kn[:, hd * ATTN_V_DIM:(hd + 1) * ATTN_V_DIM]
    vt_ref[...] = _dot_nt(w_vt_ref[...], u).astype(BF16)


def _mixer_in(h, gmix, w_in, w_vt, seg, gq, gk, pool_w, pool_scale, conv_w, w_pool_out,
              w_conv_out, *, tm):
    b, s, d = h.shape
    c = MIX_WIDTH
    tok = lambda width: pl.BlockSpec((None, tm, width), lambda bi, si: (bi, si, 0))
    head_tok = pl.BlockSpec((None, ATTN_HEADS, tm, ATTN_V_DIM), lambda bi, si: (bi, 0, si, 0))
    return pl.pallas_call(
        functools.partial(_mixer_in_kernel, tm=tm),
        grid=(b, s // tm),
        in_specs=[tok(d), _resident(gmix.shape), _resident(w_in.shape), _resident(w_vt.shape),
                  _resident(seg.shape), _resident(gq.shape), _resident(gk.shape),
                  _resident(pool_w.shape), _resident(pool_scale.shape), _resident(conv_w.shape),
                  _resident(w_pool_out.shape), _resident(w_conv_out.shape)],
        out_specs=[tok(d), tok(d), head_tok, head_tok,
                   pl.BlockSpec((None, c, tm), lambda bi, si: (bi, 0, si))],
        out_shape=[jax.ShapeDtypeStruct((b, s, d), F32),
                   jax.ShapeDtypeStruct((b, s, d), F32),
                   jax.ShapeDtypeStruct((b, ATTN_HEADS, s, ATTN_V_DIM), BF16),
                   jax.ShapeDtypeStruct((b, ATTN_HEADS, s, ATTN_V_DIM), BF16),
                   jax.ShapeDtypeStruct((b, c, s), BF16)],
        scratch_shapes=[pltpu.VMEM((POOL_HALO + tm, c), F32),
                        pltpu.VMEM((CONV_HALO + tm, c), F32)],
        compiler_params=pltpu.CompilerParams(
            dimension_semantics=("arbitrary", "arbitrary"),
            vmem_limit_bytes=VMEM_LIMIT_BYTES),
        name="mixer_in",
    )(h, gmix, w_in, w_vt, seg, gq, gk, pool_w, pool_scale, conv_w, w_pool_out, w_conv_out)


def _diff_attn_kernel(bound_ref, q_ref, k_ref, vt_ref, lam_ref, gsub_ref, o_ref,
                      p0_ref, p1_ref, acc_ref, l_ref, *, tq, tk, lam_init):
    qi = pl.program_id(2)
    q = q_ref[...]
    lane = lax.broadcasted_iota(jnp.int32, q.shape, 1)
    zero = jnp.zeros_like(q)
    q_halves = (jnp.where(lane < ATTN_HEAD_DIM, q, zero), jnp.where(lane >= ATTN_HEAD_DIM, q, zero))
    bound = bound_ref[0]
    diag = qi * tq
    acc_ref[...] = jnp.zeros_like(acc_ref)

    def chunk_mask(shape, key_chunk_offset, q_chunk_offset):
        key_chunk = lax.broadcasted_iota(jnp.int32, shape, 0) // CHUNK + key_chunk_offset
        q_chunk = lax.broadcasted_iota(jnp.int32, shape, 1) // CHUNK + q_chunk_offset
        return key_chunk <= q_chunk

    @pl.when(bound <= MAX_STATIC_SHIFT)
    def _():
        p_refs = (p0_ref, p1_ref)
        pieces = [(c, g) for c in range(2) for g in range(tq // QUERY_GROUP)]

        def step(l_sums, acc=None, probs=None):
            l_sums = list(l_sums)
            if probs is not None:
                k = k_ref[pl.ds(pl.multiple_of(probs[0], tk), tk), :]
            if acc is not None:
                vt = vt_ref[:, pl.ds(pl.multiple_of(acc[0], tk), tk)]
            for idx, (c, g) in enumerate(pieces):
                cols = slice(g * QUERY_GROUP, (g + 1) * QUERY_GROUP)
                if acc is not None:
                    acc_ref[c, :, cols] += _dot(vt, p_refs[acc[1]][c, :, cols])
                if probs is not None:
                    _, slot, key_chunk_offset = probs
                    s = _dot_nt(k, q_halves[c][cols, :])
                    p = jnp.exp2(s - bound)
                    if key_chunk_offset is not None:
                        keep = chunk_mask(s.shape, key_chunk_offset, g * (QUERY_GROUP // CHUNK))
                        p = jnp.where(keep, p, 0.0)
                    l_sums[idx] = l_sums[idx] + jnp.sum(p, axis=0, keepdims=True)
                    p_refs[slot][c, :, cols] = p.astype(BF16)
            return tuple(l_sums)

        l_sums = (jnp.zeros((1, QUERY_GROUP), F32),) * len(pieces)
        l_sums = step(l_sums, probs=(diag, 0, 0))

        def pair(r, l_sums):
            prev = jnp.where(r == 0, diag, (2 * r - 1) * tk)
            l_sums = step(l_sums, acc=(prev, 0), probs=(2 * r * tk, 1, None))
            return step(l_sums, acc=(2 * r * tk, 1), probs=((2 * r + 1) * tk, 0, None))

        l_sums = lax.fori_loop(0, qi, pair, l_sums)
        prev = jnp.where(qi == 0, diag, (2 * qi - 1) * tk)
        l_sums = step(l_sums, acc=(prev, 0), probs=(diag + tk, 1, tk // CHUNK))
        step(l_sums, acc=(diag + tk, 1))
        n_groups = tq // QUERY_GROUP
        for c in range(2):
            l_ref[c] = jnp.concatenate(l_sums[c * n_groups:(c + 1) * n_groups], axis=-1)

    @pl.when(bound > MAX_STATIC_SHIFT)
    def _():
        def tile(j, carry, masked):
            start = pl.multiple_of(j * tq, tq)
            k = k_ref[pl.ds(start, tq), :]
            vt = vt_ref[:, pl.ds(start, tq)]
            out = []
            for c in range(2):
                m, l = carry[2 * c], carry[2 * c + 1]
                s = _dot_nt(k, q_halves[c])
                if masked:
                    s = jnp.where(chunk_mask(s.shape, 0, 0), s, NEG_INF)
                m_new = jnp.maximum(m, jnp.max(s, axis=0, keepdims=True))
                alpha = jnp.exp2(m - m_new)
                p = jnp.exp2(s - m_new)
                acc_ref[c] = alpha * acc_ref[c] + _dot(vt, p.astype(BF16))
                out += [m_new, alpha * l + jnp.sum(p, axis=0, keepdims=True)]
            return tuple(out)

        init_m = jnp.full((1, tq), NEG_INF, F32)
        init_l = jnp.zeros((1, tq), F32)
        carry = lax.fori_loop(0, qi, lambda j, cr: tile(j, cr, False),
                              (init_m, init_l, init_m, init_l))
        carry = tile(qi, carry, True)
        l_ref[0] = carry[1]
        l_ref[1] = carry[3]

    lam = (jnp.exp(jnp.sum(lam_ref[0:1, :] * lam_ref[1:2, :], axis=-1, keepdims=True))
           - jnp.exp(jnp.sum(lam_ref[2:3, :] * lam_ref[3:4, :], axis=-1, keepdims=True))
           + lam_init)
    o = acc_ref[0] / l_ref[0] - lam * (acc_ref[1] / l_ref[1])
    ms = jnp.mean(o * o, axis=0, keepdims=True)
    o = o * lax.rsqrt(ms + EPS)
    o_ref[...] = (o.T * gsub_ref[...] * (1.0 - lam_init)).astype(o_ref.dtype)


def _diff_attn(bound, q, k, vt, lam_vecs, gsub, *, tq, lam_init):
    b, nh, s, dv = q.shape
    tk = tq // 2
    return pl.pallas_call(
        functools.partial(_diff_attn_kernel, tq=tq, tk=tk, lam_init=lam_init),
        grid=(b, nh, s // tq),
        in_specs=[pl.BlockSpec(memory_space=pltpu.SMEM),
                  pl.BlockSpec((None, None, tq, dv), lambda bi, hi, qi: (bi, hi, qi, 0)),
                  pl.BlockSpec((None, None, s, dv), lambda bi, hi, qi: (bi, hi, 0, 0)),
                  pl.BlockSpec((None, dv, s), lambda bi, hi, qi: (bi, hi, 0)),
                  _resident(lam_vecs.shape), _resident(gsub.shape)],
        out_specs=pl.BlockSpec((None, tq, dv), lambda bi, hi, qi: (bi, qi, hi)),
        out_shape=jax.ShapeDtypeStruct((b, s, nh * dv), BF16),
        scratch_shapes=[pltpu.VMEM((2, tk, tq), BF16), pltpu.VMEM((2, tk, tq), BF16),
                        pltpu.VMEM((2, dv, tq), F32), pltpu.VMEM((2, 1, tq), F32)],
        compiler_params=pltpu.CompilerParams(
            dimension_semantics=("arbitrary", "arbitrary", "arbitrary"),
            vmem_limit_bytes=VMEM_LIMIT_BYTES),
        name="diff_attn",
    )(bound, q, k, vt, lam_vecs, gsub)


def _mixer_out_kernel(h_ref, partial_ref, gc_ref, yc_ref, p_ref, w_attn_out_ref, w_o_ref,
                      gmlp_ref, w_up_ref, w_down_ref, gple_ref, w_gate_ref, w_proj_ref,
                      out_ref, *, ff_chunk):
    merged = partial_ref[...] + gc_ref[...] * _dot(yc_ref[...], w_attn_out_ref[...])
    h1 = h_ref[...] + _dot(merged.astype(BF16), w_o_ref[...])

    m = _rms_rows(h1, gmlp_ref[...]).astype(BF16)
    d_ff = w_up_ref.shape[1]
    mlp = jnp.zeros_like(h1)
    for lo in range(0, d_ff, ff_chunk):
        up = _dot(m, w_up_ref[:, lo:lo + ff_chunk])
        act = jnp.square(jnp.maximum(up, 0.0)).astype(BF16)
        mlp = mlp + _dot(act, w_down_ref[lo:lo + ff_chunk, :])
    h2 = h1 + mlp

    e = _rms_rows(h2, gple_ref[...]).astype(BF16)
    gate = jax.nn.sigmoid(_dot(e, w_gate_ref[...]))
    out_ref[...] = h2 + gate * _dot(p_ref[...].astype(BF16), w_proj_ref[...])


def _mixer_out(h, partial, gc, yc, p, w_attn_out, w_o, gmlp, w_up, w_down, gple, w_gate, w_proj,
               *, tm, ff_chunk):
    b, s, d = h.shape
    tok = lambda width: pl.BlockSpec((None, tm, width), lambda bi, si: (bi, si, 0))
    return pl.pallas_call(
        functools.partial(_mixer_out_kernel, ff_chunk=ff_chunk),
        grid=(b, s // tm),
        in_specs=[tok(d), tok(d), tok(d), tok(yc.shape[-1]), tok(p.shape[-1]),
                  _resident(w_attn_out.shape), _resident(w_o.shape), _resident(gmlp.shape),
                  _resident(w_up.shape), _resident(w_down.shape), _resident(gple.shape),
                  _resident(w_gate.shape), _resident(w_proj.shape)],
        out_specs=tok(d),
        out_shape=jax.ShapeDtypeStruct((b, s, d), F32),
        compiler_params=pltpu.CompilerParams(
            dimension_semantics=("arbitrary", "arbitrary"),
            vmem_limit_bytes=VMEM_LIMIT_BYTES),
        name="mixer_out",
    )(h, partial, gc, yc, p, w_attn_out, w_o, gmlp, w_up, w_down, gple, w_gate, w_proj)


def _block_diag(blocks):
    g, n, _ = blocks.shape
    eye = jnp.eye(g, dtype=blocks.dtype)
    return (eye[:, None, :, None] * blocks[:, :, None, :]).reshape(g * n, g * n)


def kernel(x, p, norm_mix_g, w_in, pool_w, pool_scale, conv_w, q_norm_g, k_norm_g, lam_q1, lam_k1,
           lam_q2, lam_k2, sub_norm_g, w_pool_out, w_conv_out, w_attn_out, w_o, norm_mlp_g, w_up,
           w_down, norm_ple_g, w_ple_gate, w_ple_proj):
    depth = w_in.shape[0]
    c = MIX_WIDTH
    tm_in, tm_out, tq, ff_chunk = 256, 256, 512, 1024
    n_seg = c // ATTN_HEAD_DIM
    seg = _block_diag(jnp.ones((n_seg, ATTN_HEAD_DIM, ATTN_HEAD_DIM), BF16))
    row = lambda v: v.reshape(1, -1).astype(F32)
    score_scale = ATTN_HEAD_DIM ** -0.5 * math.log2(math.e)

    h = x
    for i in range(depth):
        lam_init = 0.8 - 0.6 * math.exp(-0.3 * i)
        w_in_i = w_in[i]
        partial, gc, q, k, vt = _mixer_in(
            h, row(norm_mix_g[i]), w_in_i.astype(BF16),
            w_in_i[:, 6 * c:7 * c].T.astype(BF16), seg,
            row(jnp.tile(q_norm_g[i], n_seg) * score_scale), row(jnp.tile(k_norm_g[i], n_seg)),
            _block_diag(pool_w[i]).astype(BF16), row(pool_scale[i]), conv_w[i].astype(F32),
            w_pool_out[i].astype(BF16), w_conv_out[i].astype(BF16), tm=tm_in)
        lam_vecs = jnp.stack([lam_q1[i], lam_k1[i], lam_q2[i], lam_k2[i]]).astype(F32)
        score_bound = (64.0 * SCORE_BOUND_MARGIN * score_scale * jnp.max(jnp.abs(q_norm_g[i]))
                       * jnp.max(jnp.abs(k_norm_g[i]))).reshape(1).astype(F32)
        yc = _diff_attn(score_bound, q, k, vt, lam_vecs, row(sub_norm_g[i]), tq=tq,
                        lam_init=lam_init)
        h = _mixer_out(
            h, partial, gc, yc, p[i], w_attn_out[i].astype(BF16), w_o[i].astype(BF16),
            row(norm_mlp_g[i]), w_up[i].astype(BF16), w_down[i].astype(BF16),
            row(norm_ple_g[i]), w_ple_gate[i].astype(BF16), w_ple_proj[i].astype(BF16),
            tm=tm_out, ff_chunk=ff_chunk)
    return h
```

```python
import functools
import math

import jax
import jax.numpy as jnp
from jax import lax
from jax.experimental import pallas as pl
from jax.experimental.pallas import tpu as pltpu

MIX_WIDTH = 512
POOL_WINDOWS = (2, 4, 8, 16)
POOL_GROUP_DIM = MIX_WIDTH // len(POOL_WINDOWS)
CONV_WIDTH = 3
ATTN_HEADS = 4
ATTN_HEAD_DIM = 64
ATTN_V_DIM = 2 * ATTN_HEAD_DIM
CHUNK = 64
QUERY_GROUP = 256
KV_TILE = 256
GENERAL_KV_TILE = 512
MAX_STATIC_SHIFT = 30.0
SCORE_BOUND_MARGIN = 1.02
EPS = 1e-6
NEG_INF = -1e30

POOL_HALO = 16
CONV_HALO = 8
VMEM_LIMIT_BYTES = 56 * 1024 * 1024

F32 = jnp.float32
BF16 = jnp.bfloat16


def _dot(a, b):
    return jnp.dot(a, b, preferred_element_type=F32)


def _dot_nt(a, b):
    return lax.dot_general(a, b, (((1,), (1,)), ((), ())), preferred_element_type=F32)


def _rms_rows(x, g):
    ms = jnp.mean(x * x, axis=-1, keepdims=True)
    return x * lax.rsqrt(ms + EPS) * g


def _resident(shape):
    nd = len(shape)
    return pl.BlockSpec(shape, lambda *_: (0,) * nd, pipeline_mode=pl.Buffered(1))


def _mixer_in_kernel(h_ref, gmix_ref, w_in_ref, w_vt_ref, seg_ref, gq_ref, gk_ref,
                     pool_w_ref, pool_scale_ref, conv_w_ref, w_pool_out_ref, w_conv_out_ref,
                     partial_ref, gc_ref, q_ref, k_ref, vt_ref,
                     abuf_ref, zbuf_ref, *, tm):
    s_idx = pl.program_id(1)
    c = MIX_WIDTH

    @pl.when(s_idx == 0)
    def _():
        abuf_ref[0:POOL_HALO, :] = jnp.zeros((POOL_HALO, c), F32)
        zbuf_ref[0:CONV_HALO, :] = jnp.zeros((CONV_HALO, c), F32)

    u = _rms_rows(h_ref[...], gmix_ref[...]).astype(BF16)

    def proj(lo, width):
        return _dot(u, w_in_ref[:, lo:lo + width])

    a = proj(0, c)
    abuf_ref[POOL_HALO:POOL_HALO + tm, :] = a
    t_pos = (s_idx * tm + lax.broadcasted_iota(jnp.int32, (tm, POOL_GROUP_DIM), 0) + 1).astype(F32)
    pooled = []
    for g, win in enumerate(POOL_WINDOWS):
        lo = g * POOL_GROUP_DIM
        wsum = a[:, lo:lo + POOL_GROUP_DIM]
        for back in range(1, win):
            wsum = wsum + abuf_ref[POOL_HALO - back:POOL_HALO - back + tm, lo:lo + POOL_GROUP_DIM]
        cnt = jnp.minimum(t_pos, float(win))
        pooled.append(wsum / cnt - a[:, lo:lo + POOL_GROUP_DIM])
    pooled = jnp.concatenate(pooled, axis=-1).astype(BF16)
    y_a = _dot(pooled, pool_w_ref[...]) * pool_scale_ref[...]
    abuf_ref[0:POOL_HALO, :] = abuf_ref[tm:tm + POOL_HALO, :]

    z = proj(3 * c, c) * proj(c, c)
    zbuf_ref[CONV_HALO:CONV_HALO + tm, :] = z
    conv = conv_w_ref[CONV_WIDTH - 1:CONV_WIDTH, :] * z
    for back in range(1, CONV_WIDTH):
        tap = conv_w_ref[CONV_WIDTH - 1 - back:CONV_WIDTH - back, :]
        conv = conv + tap * zbuf_ref[CONV_HALO - back:CONV_HALO - back + tm, :]
    y_b = proj(2 * c, c) * conv
    zbuf_ref[0:CONV_HALO, :] = zbuf_ref[tm:tm + CONV_HALO, :]

    gate_lo = 7 * c
    d = partial_ref.shape[-1]
    g_a = jax.nn.sigmoid(proj(gate_lo, d))
    part = g_a * _dot(y_a.astype(BF16), w_pool_out_ref[...])
    g_b = jax.nn.sigmoid(proj(gate_lo + d, d))
    part = part + g_b * _dot(y_b.astype(BF16), w_conv_out_ref[...])
    partial_ref[...] = part
    gc_ref[...] = jax.nn.sigmoid(proj(gate_lo + 2 * d, d))

    def qk_norm(x, g):
        ss = _dot((x * x).astype(BF16), seg_ref[...])
        return (x * lax.rsqrt(ss * (1.0 / ATTN_HEAD_DIM) + EPS) * g).astype(BF16)

    qn = qk_norm(proj(4 * c, c), gq_ref[...])
    kn = qk_norm(proj(5 * c, c), gk_ref[...])
    for hd in range(ATTN_HEADS):
        q_ref[hd] = qn[:, hd * ATTN_V_DIM:(hd + 1) * ATTN_V_DIM]
        k_ref[hd] = kn[:, hd * ATTN_V_DIM:(hd + 1) * ATTN_V_DIM]
    vt_ref[...] = _dot_nt(w_vt_ref[...], u).astype(BF16)


def _mixer_in(h, gmix, w_in, w_vt, seg, gq, gk, pool_w, pool_scale, conv_w, w_pool_out,
              w_conv_out, *, tm):
    b, s, d = h.shape
    c = MIX_WIDTH
    tok = lambda width: pl.BlockSpec((None, tm, width), lambda bi, si: (bi, si, 0))
    head_tok = pl.BlockSpec((None, ATTN_HEADS, tm, ATTN_V_DIM), lambda bi, si: (bi, 0, si, 0))
    return pl.pallas_call(
        functools.partial(_mixer_in_kernel, tm=tm),
        grid=(b, s // tm),
        in_specs=[tok(d), _resident(gmix.shape), _resident(w_in.shape), _resident(w_vt.shape),
                  _resident(seg.shape), _resident(gq.shape), _resident(gk.shape),
                  _resident(pool_w.shape), _resident(pool_scale.shape), _resident(conv_w.shape),
                  _resident(w_pool_out.shape), _resident(w_conv_out.shape)],
        out_specs=[tok(d), tok(d), head_tok, head_tok,
                   pl.BlockSpec((None, c, tm), lambda bi, si: (bi, 0, si))],
        out_shape=[jax.ShapeDtypeStruct((b, s, d), F32),
                   jax.ShapeDtypeStruct((b, s, d), F32),
                   jax.ShapeDtypeStruct((b, ATTN_HEADS, s, ATTN_V_DIM), BF16),
                   jax.ShapeDtypeStruct((b, ATTN_HEADS, s, ATTN_V_DIM), BF16),
                   jax.ShapeDtypeStruct((b, c, s), BF16)],
        scratch_shapes=[pltpu.VMEM((POOL_HALO + tm, c), F32),
                        pltpu.VMEM((CONV_HALO + tm, c), F32)],
        compiler_params=pltpu.CompilerParams(
            dimension_semantics=("arbitrary", "arbitrary"),
            vmem_limit_bytes=VMEM_LIMIT_BYTES),
        name="mixer_in",
    )(h, gmix, w_in, w_vt, seg, gq, gk, pool_w, pool_scale, conv_w, w_pool_out, w_conv_out)


def _diff_attn_kernel(bound_ref, q_ref, k_ref, vt_ref, lam_ref, gsub_ref, o_ref,
                      pa0_ref, pa1_ref, pb0_ref, pb1_ref, acc_ref, l_ref, *, tq, lam_init):
    qi = pl.program_id(2)
    q = q_ref[...]
    lane = lax.broadcasted_iota(jnp.int32, q.shape, 1)
    zero = jnp.zeros_like(q)
    q_halves = (jnp.where(lane < ATTN_HEAD_DIM, q, zero), jnp.where(lane >= ATTN_HEAD_DIM, q, zero))
    bound = bound_ref[0]
    diag = qi * tq
    n_groups = tq // QUERY_GROUP
    acc_ref[...] = jnp.zeros_like(acc_ref)
    l_ref[...] = jnp.zeros_like(l_ref)

    def chunk_mask(shape, key_chunk_offset):
        key_chunk = lax.broadcasted_iota(jnp.int32, shape, 0) // CHUNK + key_chunk_offset
        q_chunk = lax.broadcasted_iota(jnp.int32, shape, 1) // CHUNK
        return key_chunk <= q_chunk

    @pl.when(bound <= MAX_STATIC_SHIFT)
    def _():
        tk = KV_TILE
        set_a, set_b = (pa0_ref, pa1_ref), (pb0_ref, pb1_ref)
        pieces = [(c, g) for c in range(2) for g in range(n_groups)]

        def stage(probs=None, acc=None):
            if probs is not None:
                k = k_ref[pl.ds(pl.multiple_of(probs[0], tk), tk), :]
            if acc is not None:
                vt = vt_ref[:, pl.ds(pl.multiple_of(acc[0], tk), tk)]
            for c, g in pieces:
                cols = slice(g * QUERY_GROUP, (g + 1) * QUERY_GROUP)
                if probs is not None and (probs[2] is None or g >= probs[2]):
                    s = _dot_nt(k, q_halves[c][cols, :])
                    p = jnp.exp2(s - bound)
                    if probs[2] == g:
                        p = jnp.where(chunk_mask(s.shape, 0), p, 0.0)
                    l_ref[c, :, cols] += jnp.sum(p.reshape(tk // 8, 8, QUERY_GROUP), axis=0)
                    probs[1][c, :, cols] = p.astype(BF16)
                if acc is not None and (acc[2] is None or g >= acc[2]):
                    acc_ref[c, :, cols] += _dot(vt, acc[1][c, :, cols])

        stage(probs=(diag, set_b[0], 0))
        stage(probs=(diag + tk, set_b[1], 1))
        for c in range(2):
            set_b[1][c, :, 0:QUERY_GROUP] = jnp.zeros((tk, QUERY_GROUP), BF16)

        def quad(r, carry):
            base = 4 * r * tk
            first = r == 0
            stage(probs=(base, set_a[0], None),
                  acc=(jnp.where(first, diag, base - 2 * tk), set_b[0], None))
            stage(probs=(base + tk, set_a[1], None),
                  acc=(jnp.where(first, diag + tk, base - tk), set_b[1], None))
            stage(probs=(base + 2 * tk, set_b[0], None), acc=(base, set_a[0], None))
            stage(probs=(base + 3 * tk, set_b[1], None), acc=(base + tk, set_a[1], None))
            return carry

        lax.fori_loop(0, qi, quad, 0)
        none_below = qi == 0
        stage(probs=(diag + 2 * tk, set_a[0], 2),
              acc=(jnp.where(none_below, diag, diag - 2 * tk), set_b[0], None))
        stage(probs=(diag + 3 * tk, set_a[1], 3),
              acc=(jnp.where(none_below, diag + tk, diag - tk), set_b[1], None))
        stage(acc=(diag + 2 * tk, set_a[0], 2))
        stage(acc=(diag + 3 * tk, set_a[1], 3))

    @pl.when(bound > MAX_STATIC_SHIFT)
    def _():
        tg = GENERAL_KV_TILE

        def tile(start, carry, key_chunk_offset=None):
            start = pl.multiple_of(start, tg)
            k = k_ref[pl.ds(start, tg), :]
            vt = vt_ref[:, pl.ds(start, tg)]
            out = []
            for c in range(2):
                m, l = carry[2 * c], carry[2 * c + 1]
                s = _dot_nt(k, q_halves[c])
                if key_chunk_offset is not None:
                    s = jnp.where(chunk_mask(s.shape, key_chunk_offset), s, NEG_INF)
                m_new = jnp.maximum(m, jnp.max(s, axis=0, keepdims=True))
                alpha = jnp.exp2(m - m_new)
                p = jnp.exp2(s - m_new)
                acc_ref[c] = alpha * acc_ref[c] + _dot(vt, p.astype(BF16))
                out += [m_new, alpha * l + jnp.sum(p, axis=0, keepdims=True)]
            return tuple(out)

        init_m = jnp.full((1, tq), NEG_INF, F32)
        init_l = jnp.zeros((1, tq), F32)
        carry = lax.fori_loop(0, qi * (tq // tg), lambda j, cr: tile(j * tg, cr),
                              (init_m, init_l, init_m, init_l))
        for j in range(tq // tg):
            carry = tile(diag + j * tg, carry, key_chunk_offset=j * (tg // CHUNK))
        l_ref[0, 0:1, :] = carry[1]
        l_ref[1, 0:1, :] = carry[3]

    lam = (jnp.exp(jnp.sum(lam_ref[0:1, :] * lam_ref[1:2, :], axis=-1, keepdims=True))
           - jnp.exp(jnp.sum(lam_ref[2:3, :] * lam_ref[3:4, :], axis=-1, keepdims=True))
           + lam_init)
    l1 = jnp.sum(l_ref[0], axis=0, keepdims=True)
    l2 = jnp.sum(l_ref[1], axis=0, keepdims=True)
    o = acc_ref[0] / l1 - lam * (acc_ref[1] / l2)
    ms = jnp.mean(o * o, axis=0, keepdims=True)
    o = o * lax.rsqrt(ms + EPS)
    o_ref[...] = (o.T * gsub_ref[...] * (1.0 - lam_init)).astype(o_ref.dtype)


def _diff_attn(bound, q, k, vt, lam_vecs, gsub, *, tq, lam_init):
    b, nh, s, dv = q.shape
    assert tq == 4 * KV_TILE and tq % GENERAL_KV_TILE == 0 and s % tq == 0
    p_buffer = pltpu.VMEM((2, KV_TILE, tq), BF16)
    return pl.pallas_call(
        functools.partial(_diff_attn_kernel, tq=tq, lam_init=lam_init),
        grid=(b, nh, s // tq),
        in_specs=[pl.BlockSpec(memory_space=pltpu.SMEM),
                  pl.BlockSpec((None, None, tq, dv), lambda bi, hi, qi: (bi, hi, qi, 0)),
                  pl.BlockSpec((None, None, s, dv), lambda bi, hi, qi: (bi, hi, 0, 0)),
                  pl.BlockSpec((None, dv, s), lambda bi, hi, qi: (bi, hi, 0)),
                  _resident(lam_vecs.shape), _resident(gsub.shape)],
        out_specs=pl.BlockSpec((None, tq, dv), lambda bi, hi, qi: (bi, qi, hi)),
        out_shape=jax.ShapeDtypeStruct((b, s, nh * dv), BF16),
        scratch_shapes=[p_buffer, p_buffer, p_buffer, p_buffer,
                        pltpu.VMEM((2, dv, tq), F32), pltpu.VMEM((2, 8, tq), F32)],
        compiler_params=pltpu.CompilerParams(
            dimension_semantics=("arbitrary", "arbitrary", "arbitrary"),
            vmem_limit_bytes=VMEM_LIMIT_BYTES),
        name="diff_attn",
    )(bound, q, k, vt, lam_vecs, gsub)


def _mixer_out_kernel(h_ref, partial_ref, gc_ref, yc_ref, p_ref, w_attn_out_ref, w_o_ref,
                      gmlp_ref, w_up_ref, w_down_ref, gple_ref, w_gate_ref, w_proj_ref,
                      out_ref, *, ff_chunk):
    merged = partial_ref[...] + gc_ref[...] * _dot(yc_ref[...], w_attn_out_ref[...])
    h1 = h_ref[...] + _dot(merged.astype(BF16), w_o_ref[...])

    m = _rms_rows(h1, gmlp_ref[...]).astype(BF16)
    d_ff = w_up_ref.shape[1]
    mlp = jnp.zeros_like(h1)
    for lo in range(0, d_ff, ff_chunk):
        up = _dot(m, w_up_ref[:, lo:lo + ff_chunk])
        act = jnp.square(jnp.maximum(up, 0.0)).astype(BF16)
        mlp = mlp + _dot(act, w_down_ref[lo:lo + ff_chunk, :])
    h2 = h1 + mlp

    e = _rms_rows(h2, gple_ref[...]).astype(BF16)
    gate = jax.nn.sigmoid(_dot(e, w_gate_ref[...]))
    out_ref[...] = h2 + gate * _dot(p_ref[...].astype(BF16), w_proj_ref[...])


def _mixer_out(h, partial, gc, yc, p, w_attn_out, w_o, gmlp, w_up, w_down, gple, w_gate, w_proj,
               *, tm, ff_chunk):
    b, s, d = h.shape
    tok = lambda width: pl.BlockSpec((None, tm, width), lambda bi, si: (bi, si, 0))
    return pl.pallas_call(
        functools.partial(_mixer_out_kernel, ff_chunk=ff_chunk),
        grid=(b, s // tm),
        in_specs=[tok(d), tok(d), tok(d), tok(yc.shape[-1]), tok(p.shape[-1]),
                  _resident(w_attn_out.shape), _resident(w_o.shape), _resident(gmlp.shape),
                  _resident(w_up.shape), _resident(w_down.shape), _resident(gple.shape),
                  _resident(w_gate.shape), _resident(w_proj.shape)],
        out_specs=tok(d),
        out_shape=jax.ShapeDtypeStruct((b, s, d), F32),
        compiler_params=pltpu.CompilerParams(
            dimension_semantics=("arbitrary", "arbitrary"),
            vmem_limit_bytes=VMEM_LIMIT_BYTES),
        name="mixer_out",
    )(h, partial, gc, yc, p, w_attn_out, w_o, gmlp, w_up, w_down, gple, w_gate, w_proj)


def _block_diag(blocks):
    g, n, _ = blocks.shape
    eye = jnp.eye(g, dtype=blocks.dtype)
    return (eye[:, None, :, None] * blocks[:, :, None, :]).reshape(g * n, g * n)


def kernel(x, p, norm_mix_g, w_in, pool_w, pool_scale, conv_w, q_norm_g, k_norm_g, lam_q1, lam_k1,
           lam_q2, lam_k2, sub_norm_g, w_pool_out, w_conv_out, w_attn_out, w_o, norm_mlp_g, w_up,
           w_down, norm_ple_g, w_ple_gate, w_ple_proj):
    depth = w_in.shape[0]
    c = MIX_WIDTH
    tm_in, tm_out, tq, ff_chunk = 256, 256, 1024, 1024
    n_seg = c // ATTN_HEAD_DIM
    seg = _block_diag(jnp.ones((n_seg, ATTN_HEAD_DIM, ATTN_HEAD_DIM), BF16))
    row = lambda v: v.reshape(1, -1).astype(F32)
    score_scale = ATTN_HEAD_DIM ** -0.5 * math.log2(math.e)

    h = x
    for i in range(depth):
        lam_init = 0.8 - 0.6 * math.exp(-0.3 * i)
        w_in_i = w_in[i]
        partial, gc, q, k, vt = _mixer_in(
            h, row(norm_mix_g[i]), w_in_i.astype(BF16),
            w_in_i[:, 6 * c:7 * c].T.astype(BF16), seg,
            row(jnp.tile(q_norm_g[i], n_seg) * score_scale), row(jnp.tile(k_norm_g[i], n_seg)),
            _block_diag(pool_w[i]).astype(BF16), row(pool_scale[i]), conv_w[i].astype(F32),
            w_pool_out[i].astype(BF16), w_conv_out[i].astype(BF16), tm=tm_in)
        lam_vecs = jnp.stack([lam_q1[i], lam_k1[i], lam_q2[i], lam_k2[i]]).astype(F32)
        score_bound = (64.0 * SCORE_BOUND_MARGIN * score_scale * jnp.max(jnp.abs(q_norm_g[i]))
                       * jnp.max(jnp.abs(k_norm_g[i]))).reshape(1).astype(F32)
        yc = _diff_attn(score_bound, q, k, vt, lam_vecs, row(sub_norm_g[i]), tq=tq,
                        lam_init=lam_init)
        h = _mixer_out(
            h, partial, gc, yc, p[i], w_attn_out[i].astype(BF16), w_o[i].astype(BF16),
            row(norm_mlp_g[i]), w_up[i].astype(BF16), w_down[i].astype(BF16),
            row(norm_ple_g[i]), w_ple_gate[i].astype(BF16), w_ple_proj[i].astype(BF16),
            tm=tm_out, ff_chunk=ff_chunk)
    return h
```

```python
import functools
import math

import jax
import jax.numpy as jnp
from jax import lax
from jax.experimental import pallas as pl
from jax.experimental.pallas import tpu as pltpu

MIX_WIDTH = 512
POOL_WINDOWS = (2, 4, 8, 16)
POOL_GROUP_DIM = MIX_WIDTH // len(POOL_WINDOWS)
CONV_WIDTH = 3
ATTN_HEADS = 4
ATTN_HEAD_DIM = 64
ATTN_V_DIM = 2 * ATTN_HEAD_DIM
CHUNK = 64
QUERY_GROUP = 256
KV_TILE = 256
GENERAL_KV_TILE = 512
MAX_STATIC_SHIFT = 30.0
SCORE_BOUND_MARGIN = 1.02
EPS = 1e-6
NEG_INF = -1e30

POOL_HALO = 16
CONV_HALO = 8
VMEM_LIMIT_BYTES = 56 * 1024 * 1024

F32 = jnp.float32
BF16 = jnp.bfloat16


def _dot(a, b):
    return jnp.dot(a, b, preferred_element_type=F32)


def _dot_nt(a, b):
    return lax.dot_general(a, b, (((1,), (1,)), ((), ())), preferred_element_type=F32)


def _rms_rows(x, g):
    ms = jnp.mean(x * x, axis=-1, keepdims=True)
    return x * lax.rsqrt(ms + EPS) * g


def _resident(shape):
    nd = len(shape)
    return pl.BlockSpec(shape, lambda *_: (0,) * nd, pipeline_mode=pl.Buffered(1))


def _mixer_in_kernel(h_ref, gmix_ref, w_in_ref, w_vt_ref, seg_ref, gq_ref, gk_ref,
                     pool_w_ref, pool_scale_ref, conv_w_ref, w_pool_out_ref, w_conv_out_ref,
                     partial_ref, gc_ref, q_ref, k_ref, vt_ref,
                     abuf_ref, zbuf_ref, *, tm):
    s_idx = pl.program_id(1)
    c = MIX_WIDTH

    @pl.when(s_idx == 0)
    def _():
        abuf_ref[0:POOL_HALO, :] = jnp.zeros((POOL_HALO, c), F32)
        zbuf_ref[0:CONV_HALO, :] = jnp.zeros((CONV_HALO, c), F32)

    u = _rms_rows(h_ref[...], gmix_ref[...]).astype(BF16)

    def proj(lo, width):
        return _dot(u, w_in_ref[:, lo:lo + width])

    gate_lo = 7 * c
    d = partial_ref.shape[-1]
    a = proj(0, c)
    abuf_ref[POOL_HALO:POOL_HALO + tm, :] = a
    z = proj(3 * c, c) * proj(c, c)
    zbuf_ref[CONV_HALO:CONV_HALO + tm, :] = z
    gc_ref[...] = jax.nn.sigmoid(proj(gate_lo + 2 * d, d))

    t_pos = (s_idx * tm + lax.broadcasted_iota(jnp.int32, (tm, POOL_GROUP_DIM), 0) + 1).astype(F32)
    pooled = []
    for g, win in enumerate(POOL_WINDOWS):
        lo = g * POOL_GROUP_DIM
        wsum = a[:, lo:lo + POOL_GROUP_DIM]
        for back in range(1, win):
            wsum = wsum + abuf_ref[POOL_HALO - back:POOL_HALO - back + tm, lo:lo + POOL_GROUP_DIM]
        cnt = jnp.minimum(t_pos, float(win))
        pooled.append(wsum / cnt - a[:, lo:lo + POOL_GROUP_DIM])
    pooled = jnp.concatenate(pooled, axis=-1).astype(BF16)
    y_a = _dot(pooled, pool_w_ref[...]) * pool_scale_ref[...]
    abuf_ref[0:POOL_HALO, :] = abuf_ref[tm:tm + POOL_HALO, :]

    vt_ref[...] = _dot_nt(w_vt_ref[...], u).astype(BF16)

    conv = conv_w_ref[CONV_WIDTH - 1:CONV_WIDTH, :] * z
    for back in range(1, CONV_WIDTH):
        tap = conv_w_ref[CONV_WIDTH - 1 - back:CONV_WIDTH - back, :]
        conv = conv + tap * zbuf_ref[CONV_HALO - back:CONV_HALO - back + tm, :]
    y_b = proj(2 * c, c) * conv
    zbuf_ref[0:CONV_HALO, :] = zbuf_ref[tm:tm + CONV_HALO, :]

    g_a = jax.nn.sigmoid(proj(gate_lo, d))
    part = g_a * _dot(y_a.astype(BF16), w_pool_out_ref[...])
    g_b = jax.nn.sigmoid(proj(gate_lo + d, d))
    part = part + g_b * _dot(y_b.astype(BF16), w_conv_out_ref[...])
    partial_ref[...] = part

    def qk_norm(x, g):
        ss = _dot((x * x).astype(BF16), seg_ref[...])
        return (x * lax.rsqrt(ss * (1.0 / ATTN_HEAD_DIM) + EPS) * g).astype(BF16)

    qn = qk_norm(proj(4 * c, c), gq_ref[...])
    kn = qk_norm(proj(5 * c, c), gk_ref[...])
    for hd in range(ATTN_HEADS):
        q_ref[hd] = qn[:, hd * ATTN_V_DIM:(hd + 1) * ATTN_V_DIM]
        k_ref[hd] = kn[:, hd * ATTN_V_DIM:(hd + 1) * ATTN_V_DIM]


def _mixer_in(h, gmix, w_in, w_vt, seg, gq, gk, pool_w, pool_scale, conv_w, w_pool_out,
              w_conv_out, *, tm):
    b, s, d = h.shape
    c = MIX_WIDTH
    tok = lambda width: pl.BlockSpec((None, tm, width), lambda bi, si: (bi, si, 0))
    head_tok = pl.BlockSpec((None, ATTN_HEADS, tm, ATTN_V_DIM), lambda bi, si: (bi, 0, si, 0))
    return pl.pallas_call(
        functools.partial(_mixer_in_kernel, tm=tm),
        grid=(b, s // tm),
        in_specs=[tok(d), _resident(gmix.shape), _resident(w_in.shape), _resident(w_vt.shape),
                  _resident(seg.shape), _resident(gq.shape), _resident(gk.shape),
                  _resident(pool_w.shape), _resident(pool_scale.shape), _resident(conv_w.shape),
                  _resident(w_pool_out.shape), _resident(w_conv_out.shape)],
        out_specs=[tok(d), tok(d), head_tok, head_tok,
                   pl.BlockSpec((None, c, tm), lambda bi, si: (bi, 0, si))],
        out_shape=[jax.ShapeDtypeStruct((b, s, d), F32),
                   jax.ShapeDtypeStruct((b, s, d), F32),
                   jax.ShapeDtypeStruct((b, ATTN_HEADS, s, ATTN_V_DIM), BF16),
                   jax.ShapeDtypeStruct((b, ATTN_HEADS, s, ATTN_V_DIM), BF16),
                   jax.ShapeDtypeStruct((b, c, s), BF16)],
        scratch_shapes=[pltpu.VMEM((POOL_HALO + tm, c), F32),
                        pltpu.VMEM((CONV_HALO + tm, c), F32)],
        compiler_params=pltpu.CompilerParams(
            dimension_semantics=("arbitrary", "arbitrary"),
            vmem_limit_bytes=VMEM_LIMIT_BYTES),
        name="mixer_in",
    )(h, gmix, w_in, w_vt, seg, gq, gk, pool_w, pool_scale, conv_w, w_pool_out, w_conv_out)


def _diff_attn_kernel(bound_ref, q_ref, k_ref, vt_ref, lam_ref, gsub_ref, o_ref,
                      pa0_ref, pa1_ref, pb0_ref, pb1_ref, acc_ref, l_ref, *, tq, lam_init):
    qi = pl.program_id(2)
    q = q_ref[...]
    lane = lax.broadcasted_iota(jnp.int32, q.shape, 1)
    zero = jnp.zeros_like(q)
    q_halves = (jnp.where(lane < ATTN_HEAD_DIM, q, zero), jnp.where(lane >= ATTN_HEAD_DIM, q, zero))
    bound = bound_ref[0]
    diag = qi * tq
    n_groups = tq // QUERY_GROUP
    acc_ref[...] = jnp.zeros_like(acc_ref)
    l_ref[...] = jnp.zeros_like(l_ref)

    def chunk_mask(shape, key_chunk_offset):
        key_chunk = lax.broadcasted_iota(jnp.int32, shape, 0) // CHUNK + key_chunk_offset
        q_chunk = lax.broadcasted_iota(jnp.int32, shape, 1) // CHUNK
        return key_chunk <= q_chunk

    @pl.when(bound <= MAX_STATIC_SHIFT)
    def _():
        tk = KV_TILE
        set_a, set_b = (pa0_ref, pa1_ref), (pb0_ref, pb1_ref)
        pieces = [(c, g) for c in range(2) for g in range(n_groups)]

        def stage(probs=None, acc=None):
            if probs is not None:
                k = k_ref[pl.ds(pl.multiple_of(probs[0], tk), tk), :]
            if acc is not None:
                vt = vt_ref[:, pl.ds(pl.multiple_of(acc[0], tk), tk)]
            for c, g in pieces:
                cols = slice(g * QUERY_GROUP, (g + 1) * QUERY_GROUP)
                if probs is not None and (probs[2] is None or g >= probs[2]):
                    s = _dot_nt(k, q_halves[c][cols, :])
                    p = jnp.exp2(s - bound)
                    if probs[2] == g:
                        p = jnp.where(chunk_mask(s.shape, 0), p, 0.0)
                    l_ref[c, :, cols] += jnp.sum(p.reshape(tk // 8, 8, QUERY_GROUP), axis=0)
                    probs[1][c, :, cols] = p.astype(BF16)
                if acc is not None and (acc[2] is None or g >= acc[2]):
                    acc_ref[c, :, cols] += _dot(vt, acc[1][c, :, cols])

        stage(probs=(diag, set_b[0], 0))
        stage(probs=(diag + tk, set_b[1], 1))
        for c in range(2):
            set_b[1][c, :, 0:QUERY_GROUP] = jnp.zeros((tk, QUERY_GROUP), BF16)

        def quad(r, carry):
            base = 4 * r * tk
            first = r == 0
            stage(probs=(base, set_a[0], None),
                  acc=(jnp.where(first, diag, base - 2 * tk), set_b[0], None))
            stage(probs=(base + tk, set_a[1], None),
                  acc=(jnp.where(first, diag + tk, base - tk), set_b[1], None))
            stage(probs=(base + 2 * tk, set_b[0], None), acc=(base, set_a[0], None))
            stage(probs=(base + 3 * tk, set_b[1], None), acc=(base + tk, set_a[1], None))
            return carry

        lax.fori_loop(0, qi, quad, 0)
        none_below = qi == 0
        stage(probs=(diag + 2 * tk, set_a[0], 2),
              acc=(jnp.where(none_below, diag, diag - 2 * tk), set_b[0], None))
        stage(probs=(diag + 3 * tk, set_a[1], 3),
              acc=(jnp.where(none_below, diag + tk, diag - tk), set_b[1], None))
        stage(acc=(diag + 2 * tk, set_a[0], 2))
        stage(acc=(diag + 3 * tk, set_a[1], 3))

    @pl.when(bound > MAX_STATIC_SHIFT)
    def _():
        tg = GENERAL_KV_TILE

        def tile(start, carry, key_chunk_offset=None):
            start = pl.multiple_of(start, tg)
            k = k_ref[pl.ds(start, tg), :]
            vt = vt_ref[:, pl.ds(start, tg)]
            out = []
            for c in range(2):
                m, l = carry[2 * c], carry[2 * c + 1]
                s = _dot_nt(k, q_halves[c])
                if key_chunk_offset is not None:
                    s = jnp.where(chunk_mask(s.shape, key_chunk_offset), s, NEG_INF)
                m_new = jnp.maximum(m, jnp.max(s, axis=0, keepdims=True))
                alpha = jnp.exp2(m - m_new)
                p = jnp.exp2(s - m_new)
                acc_ref[c] = alpha * acc_ref[c] + _dot(vt, p.astype(BF16))
                out += [m_new, alpha * l + jnp.sum(p, axis=0, keepdims=True)]
            return tuple(out)

        init_m = jnp.full((1, tq), NEG_INF, F32)
        init_l = jnp.zeros((1, tq), F32)
        carry = lax.fori_loop(0, qi * (tq // tg), lambda j, cr: tile(j * tg, cr),
                              (init_m, init_l, init_m, init_l))
        for j in range(tq // tg):
            carry = tile(diag + j * tg, carry, key_chunk_offset=j * (tg // CHUNK))
        l_ref[0, 0:1, :] = carry[1]
        l_ref[1, 0:1, :] = carry[3]

    lam = (jnp.exp(jnp.sum(lam_ref[0:1, :] * lam_ref[1:2, :], axis=-1, keepdims=True))
           - jnp.exp(jnp.sum(lam_ref[2:3, :] * lam_ref[3:4, :], axis=-1, keepdims=True))
           + lam_init)
    l1 = jnp.sum(l_ref[0], axis=0, keepdims=True)
    l2 = jnp.sum(l_ref[1], axis=0, keepdims=True)
    o = acc_ref[0] / l1 - lam * (acc_ref[1] / l2)
    ms = jnp.mean(o * o, axis=0, keepdims=True)
    o = o * lax.rsqrt(ms + EPS)
    o_ref[...] = (o.T * gsub_ref[...] * (1.0 - lam_init)).astype(o_ref.dtype)


def _diff_attn(bound, q, k, vt, lam_vecs, gsub, *, tq, lam_init):
    b, nh, s, dv = q.shape
    assert tq == 4 * KV_TILE and tq % GENERAL_KV_TILE == 0 and s % tq == 0
    p_buffer = pltpu.VMEM((2, KV_TILE, tq), BF16)
    return pl.pallas_call(
        functools.partial(_diff_attn_kernel, tq=tq, lam_init=lam_init),
        grid=(b, nh, s // tq),
        in_specs=[pl.BlockSpec(memory_space=pltpu.SMEM),
                  pl.BlockSpec((None, None, tq, dv), lambda bi, hi, qi: (bi, hi, qi, 0)),
                  pl.BlockSpec((None, None, s, dv), lambda bi, hi, qi: (bi, hi, 0, 0)),
                  pl.BlockSpec((None, dv, s), lambda bi, hi, qi: (bi, hi, 0)),
                  _resident(lam_vecs.shape), _resident(gsub.shape)],
        out_specs=pl.BlockSpec((None, tq, dv), lambda bi, hi, qi: (bi, qi, hi)),
        out_shape=jax.ShapeDtypeStruct((b, s, nh * dv), BF16),
        scratch_shapes=[p_buffer, p_buffer, p_buffer, p_buffer,
                        pltpu.VMEM((2, dv, tq), F32), pltpu.VMEM((2, 8, tq), F32)],
        compiler_params=pltpu.CompilerParams(
            dimension_semantics=("arbitrary", "arbitrary", "arbitrary"),
            vmem_limit_bytes=VMEM_LIMIT_BYTES),
        name="diff_attn",
    )(bound, q, k, vt, lam_vecs, gsub)


def _mixer_out_kernel(h_ref, partial_ref, gc_ref, yc_ref, p_ref, w_attn_out_ref, w_o_ref,
                      gmlp_ref, w_up_ref, w_down_ref, gple_ref, w_gate_ref, w_proj_ref,
                      out_ref, *, ff_chunk):
    merged = partial_ref[...] + gc_ref[...] * _dot(yc_ref[...], w_attn_out_ref[...])
    h1 = h_ref[...] + _dot(merged.astype(BF16), w_o_ref[...])

    m = _rms_rows(h1, gmlp_ref[...]).astype(BF16)
    d_ff = w_up_ref.shape[1]
    mlp = jnp.zeros_like(h1)
    for lo in range(0, d_ff, ff_chunk):
        up = _dot(m, w_up_ref[:, lo:lo + ff_chunk])
        act = jnp.square(jnp.maximum(up, 0.0)).astype(BF16)
        mlp = mlp + _dot(act, w_down_ref[lo:lo + ff_chunk, :])
    h2 = h1 + mlp

    e = _rms_rows(h2, gple_ref[...]).astype(BF16)
    gate = jax.nn.sigmoid(_dot(e, w_gate_ref[...]))
    out_ref[...] = h2 + gate * _dot(p_ref[...].astype(BF16), w_proj_ref[...])


def _mixer_out(h, partial, gc, yc, p, w_attn_out, w_o, gmlp, w_up, w_down, gple, w_gate, w_proj,
               *, layer, tm, ff_chunk):
    b, s, d = h.shape
    tok = lambda width: pl.BlockSpec((None, tm, width), lambda bi, si: (bi, si, 0))
    p_tok = pl.BlockSpec((None, None, tm, p.shape[-1]), lambda bi, si: (layer, bi, si, 0))
    return pl.pallas_call(
        functools.partial(_mixer_out_kernel, ff_chunk=ff_chunk),
        grid=(b, s // tm),
        in_specs=[tok(d), tok(d), tok(d), tok(yc.shape[-1]), p_tok,
                  _resident(w_attn_out.shape), _resident(w_o.shape), _resident(gmlp.shape),
                  _resident(w_up.shape), _resident(w_down.shape), _resident(gple.shape),
                  _resident(w_gate.shape), _resident(w_proj.shape)],
        out_specs=tok(d),
        out_shape=jax.ShapeDtypeStruct((b, s, d), F32),
        compiler_params=pltpu.CompilerParams(
            dimension_semantics=("arbitrary", "arbitrary"),
            vmem_limit_bytes=VMEM_LIMIT_BYTES),
        name="mixer_out",
    )(h, partial, gc, yc, p, w_attn_out, w_o, gmlp, w_up, w_down, gple, w_gate, w_proj)


def _block_diag(blocks):
    g, n, _ = blocks.shape
    eye = jnp.eye(g, dtype=blocks.dtype)
    return (eye[:, None, :, None] * blocks[:, :, None, :]).reshape(g * n, g * n)


def kernel(x, p, norm_mix_g, w_in, pool_w, pool_scale, conv_w, q_norm_g, k_norm_g, lam_q1, lam_k1,
           lam_q2, lam_k2, sub_norm_g, w_pool_out, w_conv_out, w_attn_out, w_o, norm_mlp_g, w_up,
           w_down, norm_ple_g, w_ple_gate, w_ple_proj):
    depth = w_in.shape[0]
    c = MIX_WIDTH
    tm_in, tm_out, tq, ff_chunk = 512, 512, 1024, 1024
    n_seg = c // ATTN_HEAD_DIM
    seg = _block_diag(jnp.ones((n_seg, ATTN_HEAD_DIM, ATTN_HEAD_DIM), BF16))
    row = lambda v: v.reshape(1, -1).astype(F32)
    score_scale = ATTN_HEAD_DIM ** -0.5 * math.log2(math.e)

    h = x
    for i in range(depth):
        lam_init = 0.8 - 0.6 * math.exp(-0.3 * i)
        w_in_i = w_in[i]
        partial, gc, q, k, vt = _mixer_in(
            h, row(norm_mix_g[i]), w_in_i.astype(BF16),
            w_in_i[:, 6 * c:7 * c].T.astype(BF16), seg,
            row(jnp.tile(q_norm_g[i], n_seg) * score_scale), row(jnp.tile(k_norm_g[i], n_seg)),
            _block_diag(pool_w[i]).astype(BF16), row(pool_scale[i]), conv_w[i].astype(F32),
            w_pool_out[i].astype(BF16), w_conv_out[i].astype(BF16), tm=tm_in)
        lam_vecs = jnp.stack([lam_q1[i], lam_k1[i], lam_q2[i], lam_k2[i]]).astype(F32)
        score_bound = (64.0 * SCORE_BOUND_MARGIN * score_scale * jnp.max(jnp.abs(q_norm_g[i]))
                       * jnp.max(jnp.abs(k_norm_g[i]))).reshape(1).astype(F32)
        yc = _diff_attn(score_bound, q, k, vt, lam_vecs, row(sub_norm_g[i]), tq=tq,
                        lam_init=lam_init)
        h = _mixer_out(
            h, partial, gc, yc, p, w_attn_out[i].astype(BF16), w_o[i].astype(BF16),
            row(norm_mlp_g[i]), w_up[i].astype(BF16), w_down[i].astype(BF16),
            row(norm_ple_g[i]), w_ple_gate[i].astype(BF16), w_ple_proj[i].astype(BF16),
            layer=i, tm=tm_out, ff_chunk=ff_chunk)
    return h
```

```python
import functools
import math

import jax
import jax.numpy as jnp
from jax import lax
from jax.experimental import pallas as pl
from jax.experimental.pallas import tpu as pltpu

MIX_WIDTH = 512
POOL_WINDOWS = (2, 4, 8, 16)
POOL_GROUP_DIM = MIX_WIDTH // len(POOL_WINDOWS)
CONV_WIDTH = 3
ATTN_HEADS = 4
ATTN_HEAD_DIM = 64
ATTN_V_DIM = 2 * ATTN_HEAD_DIM
CHUNK = 64
QUERY_GROUP = 256
KV_TILE = 256
GENERAL_KV_TILE = 256
MAX_STATIC_SHIFT = 30.0
SCORE_BOUND_MARGIN = 1.02
EPS = 1e-6
NEG_INF = -1e30

POOL_HALO = 16
CONV_HALO = 8
VMEM_LIMIT_BYTES = 56 * 1024 * 1024

F32 = jnp.float32
BF16 = jnp.bfloat16


def _dot(a, b):
    return jnp.dot(a, b, preferred_element_type=F32)


def _dot_nt(a, b):
    return lax.dot_general(a, b, (((1,), (1,)), ((), ())), preferred_element_type=F32)


def _rms_rows(x, g):
    ms = jnp.mean(x * x, axis=-1, keepdims=True)
    return x * lax.rsqrt(ms + EPS) * g


def _resident(shape):
    nd = len(shape)
    return pl.BlockSpec(shape, lambda *_: (0,) * nd, pipeline_mode=pl.Buffered(1))


def _mixer_in_kernel(h_ref, gmix_ref, w_in_ref, w_vt_ref, seg_ref, gq_ref, gk_ref,
                     pool_w_ref, pool_scale_ref, conv_w_ref, w_pool_out_ref, w_conv_out_ref,
                     partial_ref, gc_ref, q_ref, k_ref, vt_ref,
                     abuf_ref, zbuf_ref, *, tm):
    s_idx = pl.program_id(1)
    c = MIX_WIDTH

    @pl.when(s_idx == 0)
    def _():
        abuf_ref[0:POOL_HALO, :] = jnp.zeros((POOL_HALO, c), F32)
        zbuf_ref[0:CONV_HALO, :] = jnp.zeros((CONV_HALO, c), F32)

    u = _rms_rows(h_ref[...], gmix_ref[...]).astype(BF16)

    def proj(lo, width):
        return _dot(u, w_in_ref[:, lo:lo + width])

    gate_lo = 7 * c
    d = partial_ref.shape[-1]
    a = proj(0, c)
    abuf_ref[POOL_HALO:POOL_HALO + tm, :] = a
    z = proj(3 * c, c) * proj(c, c)
    zbuf_ref[CONV_HALO:CONV_HALO + tm, :] = z
    gc_ref[...] = jax.nn.sigmoid(proj(gate_lo + 2 * d, d))

    t_pos = (s_idx * tm + lax.broadcasted_iota(jnp.int32, (tm, POOL_GROUP_DIM), 0) + 1).astype(F32)
    pooled = []
    for g, win in enumerate(POOL_WINDOWS):
        lo = g * POOL_GROUP_DIM
        wsum = a[:, lo:lo + POOL_GROUP_DIM]
        for back in range(1, win):
            wsum = wsum + abuf_ref[POOL_HALO - back:POOL_HALO - back + tm, lo:lo + POOL_GROUP_DIM]
        cnt = jnp.minimum(t_pos, float(win))
        pooled.append(wsum / cnt - a[:, lo:lo + POOL_GROUP_DIM])
    pooled = jnp.concatenate(pooled, axis=-1).astype(BF16)
    y_a = _dot(pooled, pool_w_ref[...]) * pool_scale_ref[...]
    abuf_ref[0:POOL_HALO, :] = abuf_ref[tm:tm + POOL_HALO, :]

    vt_ref[...] = _dot_nt(w_vt_ref[...], u).astype(BF16)

    conv = conv_w_ref[CONV_WIDTH - 1:CONV_WIDTH, :] * z
    for back in range(1, CONV_WIDTH):
        tap = conv_w_ref[CONV_WIDTH - 1 - back:CONV_WIDTH - back, :]
        conv = conv + tap * zbuf_ref[CONV_HALO - back:CONV_HALO - back + tm, :]
    y_b = proj(2 * c, c) * conv
    zbuf_ref[0:CONV_HALO, :] = zbuf_ref[tm:tm + CONV_HALO, :]

    g_a = jax.nn.sigmoid(proj(gate_lo, d))
    part = g_a * _dot(y_a.astype(BF16), w_pool_out_ref[...])
    g_b = jax.nn.sigmoid(proj(gate_lo + d, d))
    part = part + g_b * _dot(y_b.astype(BF16), w_conv_out_ref[...])
    partial_ref[...] = part

    def qk_norm(x, g):
        ss = _dot((x * x).astype(BF16), seg_ref[...])
        return (x * lax.rsqrt(ss * (1.0 / ATTN_HEAD_DIM) + EPS) * g).astype(BF16)

    qn = qk_norm(proj(4 * c, c), gq_ref[...])
    kn = qk_norm(proj(5 * c, c), gk_ref[...])
    for hd in range(ATTN_HEADS):
        q_ref[hd] = qn[:, hd * ATTN_V_DIM:(hd + 1) * ATTN_V_DIM]
        k_ref[hd] = kn[:, hd * ATTN_V_DIM:(hd + 1) * ATTN_V_DIM]


def _mixer_in(h, gmix, w_in, w_vt, seg, gq, gk, pool_w, pool_scale, conv_w, w_pool_out,
              w_conv_out, *, tm):
    b, s, d = h.shape
    c = MIX_WIDTH
    tok = lambda width: pl.BlockSpec((None, tm, width), lambda bi, si: (bi, si, 0))
    head_tok = pl.BlockSpec((None, ATTN_HEADS, tm, ATTN_V_DIM), lambda bi, si: (bi, 0, si, 0))
    return pl.pallas_call(
        functools.partial(_mixer_in_kernel, tm=tm),
        grid=(b, s // tm),
        in_specs=[tok(d), _resident(gmix.shape), _resident(w_in.shape), _resident(w_vt.shape),
                  _resident(seg.shape), _resident(gq.shape), _resident(gk.shape),
                  _resident(pool_w.shape), _resident(pool_scale.shape), _resident(conv_w.shape),
                  _resident(w_pool_out.shape), _resident(w_conv_out.shape)],
        out_specs=[tok(d), tok(d), head_tok, head_tok,
                   pl.BlockSpec((None, c, tm), lambda bi, si: (bi, 0, si))],
        out_shape=[jax.ShapeDtypeStruct((b, s, d), F32),
                   jax.ShapeDtypeStruct((b, s, d), F32),
                   jax.ShapeDtypeStruct((b, ATTN_HEADS, s, ATTN_V_DIM), BF16),
                   jax.ShapeDtypeStruct((b, ATTN_HEADS, s, ATTN_V_DIM), BF16),
                   jax.ShapeDtypeStruct((b, c, s), BF16)],
        scratch_shapes=[pltpu.VMEM((POOL_HALO + tm, c), F32),
                        pltpu.VMEM((CONV_HALO + tm, c), F32)],
        compiler_params=pltpu.CompilerParams(
            dimension_semantics=("arbitrary", "arbitrary"),
            vmem_limit_bytes=VMEM_LIMIT_BYTES),
        name="mixer_in",
    )(h, gmix, w_in, w_vt, seg, gq, gk, pool_w, pool_scale, conv_w, w_pool_out, w_conv_out)


def _diff_attn_kernel(bound_ref, q_ref, k_ref, vt_ref, lam_ref, gsub_ref, o_ref,
                      pa0_ref, pa1_ref, pb0_ref, pb1_ref, acc_ref, l_ref, *, tq, lam_init):
    qi = pl.program_id(2)
    q = q_ref[...]
    lane = lax.broadcasted_iota(jnp.int32, q.shape, 1)
    zero = jnp.zeros_like(q)
    q_halves = (jnp.where(lane < ATTN_HEAD_DIM, q, zero), jnp.where(lane >= ATTN_HEAD_DIM, q, zero))
    bound = bound_ref[0]
    diag = qi * tq
    n_groups = tq // QUERY_GROUP
    acc_ref[...] = jnp.zeros_like(acc_ref)
    l_ref[...] = jnp.zeros_like(l_ref)

    def chunk_mask(shape, key_chunk_offset):
        key_chunk = lax.broadcasted_iota(jnp.int32, shape, 0) // CHUNK + key_chunk_offset
        q_chunk = lax.broadcasted_iota(jnp.int32, shape, 1) // CHUNK
        return key_chunk <= q_chunk

    @pl.when(bound <= MAX_STATIC_SHIFT)
    def _():
        tk = KV_TILE
        set_a, set_b = (pa0_ref, pa1_ref), (pb0_ref, pb1_ref)
        pieces = [(c, g) for c in range(2) for g in range(n_groups)]

        def stage(probs=None, acc=None):
            if probs is not None:
                k = k_ref[pl.ds(pl.multiple_of(probs[0], tk), tk), :]
            if acc is not None:
                vt = vt_ref[:, pl.ds(pl.multiple_of(acc[0], tk), tk)]
            for c, g in pieces:
                cols = slice(g * QUERY_GROUP, (g + 1) * QUERY_GROUP)
                if probs is not None and (probs[2] is None or g >= probs[2]):
                    s = _dot_nt(k, q_halves[c][cols, :])
                    p = jnp.exp2(s - bound)
                    if probs[2] == g:
                        p = jnp.where(chunk_mask(s.shape, 0), p, 0.0)
                    l_ref[c, :, cols] += jnp.sum(p.reshape(tk // 8, 8, QUERY_GROUP), axis=0)
                    probs[1][c, :, cols] = p.astype(BF16)
                if acc is not None and (acc[2] is None or g >= acc[2]):
                    acc_ref[c, :, cols] += _dot(vt, acc[1][c, :, cols])

        stage(probs=(diag, set_b[0], 0))
        stage(probs=(diag + tk, set_b[1], 1))
        for c in range(2):
            set_b[1][c, :, 0:QUERY_GROUP] = jnp.zeros((tk, QUERY_GROUP), BF16)

        def quad(r, carry):
            base = 4 * r * tk
            first = r == 0
            stage(probs=(base, set_a[0], None),
                  acc=(jnp.where(first, diag, base - 2 * tk), set_b[0], None))
            stage(probs=(base + tk, set_a[1], None),
                  acc=(jnp.where(first, diag + tk, base - tk), set_b[1], None))
            stage(probs=(base + 2 * tk, set_b[0], None), acc=(base, set_a[0], None))
            stage(probs=(base + 3 * tk, set_b[1], None), acc=(base + tk, set_a[1], None))
            return carry

        n_diag = tq // tk
        lax.fori_loop(0, qi * n_diag // 4, quad, 0)
        none_below = qi == 0
        pending = [(jnp.where(none_below, diag, diag - 2 * tk), set_b[0], None),
                   (jnp.where(none_below, diag + tk, diag - tk), set_b[1], None)]
        free, busy = set_a, set_b
        for j in range(2, n_diag, 2):
            filled = [(diag + (j + i) * tk, free[i], j + i) for i in range(2)]
            for i in range(2):
                stage(probs=filled[i], acc=pending[i])
            pending, free, busy = filled, busy, free
        for i in range(2):
            stage(acc=pending[i])

    @pl.when(bound > MAX_STATIC_SHIFT)
    def _():
        tg = GENERAL_KV_TILE

        def tile(start, carry, key_chunk_offset=None):
            start = pl.multiple_of(start, tg)
            k = k_ref[pl.ds(start, tg), :]
            vt = vt_ref[:, pl.ds(start, tg)]
            out = []
            for c in range(2):
                m, l = carry[2 * c], carry[2 * c + 1]
                s = _dot_nt(k, q_halves[c])
                if key_chunk_offset is not None:
                    s = jnp.where(chunk_mask(s.shape, key_chunk_offset), s, NEG_INF)
                m_new = jnp.maximum(m, jnp.max(s, axis=0, keepdims=True))
                alpha = jnp.exp2(m - m_new)
                p = jnp.exp2(s - m_new)
                acc_ref[c] = alpha * acc_ref[c] + _dot(vt, p.astype(BF16))
                out += [m_new, alpha * l + jnp.sum(p, axis=0, keepdims=True)]
            return tuple(out)

        init_m = jnp.full((1, tq), NEG_INF, F32)
        init_l = jnp.zeros((1, tq), F32)
        carry = lax.fori_loop(0, qi * (tq // tg), lambda j, cr: tile(j * tg, cr),
                              (init_m, init_l, init_m, init_l))
        for j in range(tq // tg):
            carry = tile(diag + j * tg, carry, key_chunk_offset=j * (tg // CHUNK))
        l_ref[0, 0:1, :] = carry[1]
        l_ref[1, 0:1, :] = carry[3]

    lam = (jnp.exp(jnp.sum(lam_ref[0:1, :] * lam_ref[1:2, :], axis=-1, keepdims=True))
           - jnp.exp(jnp.sum(lam_ref[2:3, :] * lam_ref[3:4, :], axis=-1, keepdims=True))
           + lam_init)
    l1 = jnp.sum(l_ref[0], axis=0, keepdims=True)
    l2 = jnp.sum(l_ref[1], axis=0, keepdims=True)
    o = acc_ref[0] / l1 - lam * (acc_ref[1] / l2)
    ms = jnp.mean(o * o, axis=0, keepdims=True)
    o = o * lax.rsqrt(ms + EPS)
    o_ref[...] = (o.T * gsub_ref[...] * (1.0 - lam_init)).astype(o_ref.dtype)


def _diff_attn(bound, q, k, vt, lam_vecs, gsub, *, tq, lam_init):
    b, nh, s, dv = q.shape
    assert tq % (4 * KV_TILE) == 0 and tq % GENERAL_KV_TILE == 0 and s % tq == 0
    p_buffer = pltpu.VMEM((2, KV_TILE, tq), BF16)
    return pl.pallas_call(
        functools.partial(_diff_attn_kernel, tq=tq, lam_init=lam_init),
        grid=(b, nh, s // tq),
        in_specs=[pl.BlockSpec(memory_space=pltpu.SMEM),
                  pl.BlockSpec((None, None, tq, dv), lambda bi, hi, qi: (bi, hi, qi, 0)),
                  pl.BlockSpec((None, None, s, dv), lambda bi, hi, qi: (bi, hi, 0, 0)),
                  pl.BlockSpec((None, dv, s), lambda bi, hi, qi: (bi, hi, 0)),
                  _resident(lam_vecs.shape), _resident(gsub.shape)],
        out_specs=pl.BlockSpec((None, tq, dv), lambda bi, hi, qi: (bi, qi, hi)),
        out_shape=jax.ShapeDtypeStruct((b, s, nh * dv), BF16),
        scratch_shapes=[p_buffer, p_buffer, p_buffer, p_buffer,
                        pltpu.VMEM((2, dv, tq), F32), pltpu.VMEM((2, 8, tq), F32)],
        compiler_params=pltpu.CompilerParams(
            dimension_semantics=("arbitrary", "arbitrary", "arbitrary"),
            vmem_limit_bytes=VMEM_LIMIT_BYTES),
        name="diff_attn",
    )(bound, q, k, vt, lam_vecs, gsub)


def _mixer_out_kernel(h_ref, partial_ref, gc_ref, yc_ref, p_ref, w_attn_out_ref, w_o_ref,
                      gmlp_ref, w_up_ref, w_down_ref, gple_ref, w_gate_ref, w_proj_ref,
                      out_ref, *, ff_chunk):
    merged = partial_ref[...] + gc_ref[...] * _dot(yc_ref[...], w_attn_out_ref[...])
    h1 = h_ref[...] + _dot(merged.astype(BF16), w_o_ref[...])
    embed = _dot(p_ref[...].astype(BF16), w_proj_ref[...])

    m = _rms_rows(h1, gmlp_ref[...]).astype(BF16)
    d_ff = w_up_ref.shape[1]
    mlp = jnp.zeros_like(h1)
    for lo in range(0, d_ff, ff_chunk):
        up = _dot(m, w_up_ref[:, lo:lo + ff_chunk])
        act = jnp.square(jnp.maximum(up, 0.0)).astype(BF16)
        mlp = mlp + _dot(act, w_down_ref[lo:lo + ff_chunk, :])
    h2 = h1 + mlp

    e = _rms_rows(h2, gple_ref[...]).astype(BF16)
    gate = jax.nn.sigmoid(_dot(e, w_gate_ref[...]))
    out_ref[...] = h2 + gate * embed


def _mixer_out(h, partial, gc, yc, p, w_attn_out, w_o, gmlp, w_up, w_down, gple, w_gate, w_proj,
               *, layer, tm, ff_chunk):
    b, s, d = h.shape
    tok = lambda width: pl.BlockSpec((None, tm, width), lambda bi, si: (bi, si, 0))
    p_tok = pl.BlockSpec((None, None, tm, p.shape[-1]), lambda bi, si: (layer, bi, si, 0))
    return pl.pallas_call(
        functools.partial(_mixer_out_kernel, ff_chunk=ff_chunk),
        grid=(b, s // tm),
        in_specs=[tok(d), tok(d), tok(d), tok(yc.shape[-1]), p_tok,
                  _resident(w_attn_out.shape), _resident(w_o.shape), _resident(gmlp.shape),
                  _resident(w_up.shape), _resident(w_down.shape), _resident(gple.shape),
                  _resident(w_gate.shape), _resident(w_proj.shape)],
        out_specs=tok(d),
        out_shape=jax.ShapeDtypeStruct((b, s, d), F32),
        compiler_params=pltpu.CompilerParams(
            dimension_semantics=("arbitrary", "arbitrary"),
            vmem_limit_bytes=VMEM_LIMIT_BYTES),
        name="mixer_out",
    )(h, partial, gc, yc, p, w_attn_out, w_o, gmlp, w_up, w_down, gple, w_gate, w_proj)


def _block_diag(blocks):
    g, n, _ = blocks.shape
    eye = jnp.eye(g, dtype=blocks.dtype)
    return (eye[:, None, :, None] * blocks[:, :, None, :]).reshape(g * n, g * n)


def kernel(x, p, norm_mix_g, w_in, pool_w, pool_scale, conv_w, q_norm_g, k_norm_g, lam_q1, lam_k1,
           lam_q2, lam_k2, sub_norm_g, w_pool_out, w_conv_out, w_attn_out, w_o, norm_mlp_g, w_up,
           w_down, norm_ple_g, w_ple_gate, w_ple_proj):
    depth = w_in.shape[0]
    c = MIX_WIDTH
    tm_in, tm_out, tq, ff_chunk = 512, 512, 2048, 1024
    n_seg = c // ATTN_HEAD_DIM
    seg = _block_diag(jnp.ones((n_seg, ATTN_HEAD_DIM, ATTN_HEAD_DIM), BF16))
    row = lambda v: v.reshape(1, -1).astype(F32)
    score_scale = ATTN_HEAD_DIM ** -0.5 * math.log2(math.e)

    h = x
    for i in range(depth):
        lam_init = 0.8 - 0.6 * math.exp(-0.3 * i)
        w_in_i = w_in[i]
        partial, gc, q, k, vt = _mixer_in(
            h, row(norm_mix_g[i]), w_in_i.astype(BF16),
            w_in_i[:, 6 * c:7 * c].T.astype(BF16), seg,
            row(jnp.tile(q_norm_g[i], n_seg) * score_scale), row(jnp.tile(k_norm_g[i], n_seg)),
            _block_diag(pool_w[i]).astype(BF16), row(pool_scale[i]), conv_w[i].astype(F32),
            w_pool_out[i].astype(BF16), w_conv_out[i].astype(BF16), tm=tm_in)
        lam_vecs = jnp.stack([lam_q1[i], lam_k1[i], lam_q2[i], lam_k2[i]]).astype(F32)
        score_bound = (64.0 * SCORE_BOUND_MARGIN * score_scale * jnp.max(jnp.abs(q_norm_g[i]))
                       * jnp.max(jnp.abs(k_norm_g[i]))).reshape(1).astype(F32)
        yc = _diff_attn(score_bound, q, k, vt, lam_vecs, row(sub_norm_g[i]), tq=tq,
                        lam_init=lam_init)
        h = _mixer_out(
            h, partial, gc, yc, p, w_attn_out[i].astype(BF16), w_o[i].astype(BF16),
            row(norm_mlp_g[i]), w_up[i].astype(BF16), w_down[i].astype(BF16),
            row(norm_ple_g[i]), w_ple_gate[i].astype(BF16), w_ple_proj[i].astype(BF16),
            layer=i, tm=tm_out, ff_chunk=ff_chunk)
    return h
```

```python
import functools
import math

import jax
import jax.numpy as jnp
from jax import lax
from jax.experimental import pallas as pl
from jax.experimental.pallas import tpu as pltpu

MIX_WIDTH = 512
POOL_WINDOWS = (2, 4, 8, 16)
POOL_GROUP_DIM = MIX_WIDTH // len(POOL_WINDOWS)
CONV_WIDTH = 3
ATTN_HEADS = 4
ATTN_HEAD_DIM = 64
ATTN_V_DIM = 2 * ATTN_HEAD_DIM
CHUNK = 64
QUERY_GROUP = 256
KV_TILE = 256
GENERAL_KV_TILE = 256
MAX_STATIC_SHIFT = 30.0
SCORE_BOUND_MARGIN = 1.02
EPS = 1e-6
NEG_INF = -1e30

POOL_HALO = 16
CONV_HALO = 8
VMEM_LIMIT_BYTES = 56 * 1024 * 1024

F32 = jnp.float32
BF16 = jnp.bfloat16


def _dot(a, b):
    return jnp.dot(a, b, preferred_element_type=F32)


def _dot_nt(a, b):
    return lax.dot_general(a, b, (((1,), (1,)), ((), ())), preferred_element_type=F32)


def _rms_rows(x, g):
    ms = jnp.mean(x * x, axis=-1, keepdims=True)
    return x * lax.rsqrt(ms + EPS) * g


def _resident(shape):
    nd = len(shape)
    return pl.BlockSpec(shape, lambda *_: (0,) * nd, pipeline_mode=pl.Buffered(1))


def _layer_resident(stacked_shape, layer):
    nd = len(stacked_shape)
    return pl.BlockSpec((None,) + tuple(stacked_shape[1:]), lambda *_: (layer,) + (0,) * (nd - 1),
                        pipeline_mode=pl.Buffered(1))


def _mixer_in_kernel(h_ref, gmix_ref, w_in_ref, w_vt_ref, seg_ref, gq_ref, gk_ref,
                     pool_w_ref, pool_scale_ref, conv_w_ref, w_pool_out_ref, w_conv_out_ref,
                     partial_ref, gc_ref, q_ref, k_ref, vt_ref,
                     abuf_ref, zbuf_ref, *, tm):
    s_idx = pl.program_id(1)
    c = MIX_WIDTH

    @pl.when(s_idx == 0)
    def _():
        abuf_ref[0:POOL_HALO, :] = jnp.zeros((POOL_HALO, c), F32)
        zbuf_ref[0:CONV_HALO, :] = jnp.zeros((CONV_HALO, c), F32)

    u = _rms_rows(h_ref[...], gmix_ref[...]).astype(BF16)

    def proj(lo, width):
        return _dot(u, w_in_ref[:, lo:lo + width])

    gate_lo = 7 * c
    d = partial_ref.shape[-1]
    a = proj(0, c)
    abuf_ref[POOL_HALO:POOL_HALO + tm, :] = a
    z = proj(3 * c, c) * proj(c, c)
    zbuf_ref[CONV_HALO:CONV_HALO + tm, :] = z
    gc_ref[...] = jax.nn.sigmoid(proj(gate_lo + 2 * d, d))

    t_pos = (s_idx * tm + lax.broadcasted_iota(jnp.int32, (tm, POOL_GROUP_DIM), 0) + 1).astype(F32)
    pooled = []
    for g, win in enumerate(POOL_WINDOWS):
        lo = g * POOL_GROUP_DIM
        wsum = a[:, lo:lo + POOL_GROUP_DIM]
        for back in range(1, win):
            wsum = wsum + abuf_ref[POOL_HALO - back:POOL_HALO - back + tm, lo:lo + POOL_GROUP_DIM]
        cnt = jnp.minimum(t_pos, float(win))
        pooled.append(wsum / cnt - a[:, lo:lo + POOL_GROUP_DIM])
    pooled = jnp.concatenate(pooled, axis=-1).astype(BF16)
    y_a = _dot(pooled, pool_w_ref[...]) * pool_scale_ref[...]
    abuf_ref[0:POOL_HALO, :] = abuf_ref[tm:tm + POOL_HALO, :]

    vt_ref[...] = _dot_nt(w_vt_ref[...], u).astype(BF16)

    conv = conv_w_ref[CONV_WIDTH - 1:CONV_WIDTH, :] * z
    for back in range(1, CONV_WIDTH):
        tap = conv_w_ref[CONV_WIDTH - 1 - back:CONV_WIDTH - back, :]
        conv = conv + tap * zbuf_ref[CONV_HALO - back:CONV_HALO - back + tm, :]
    y_b = proj(2 * c, c) * conv
    zbuf_ref[0:CONV_HALO, :] = zbuf_ref[tm:tm + CONV_HALO, :]

    g_a = jax.nn.sigmoid(proj(gate_lo, d))
    part = g_a * _dot(y_a.astype(BF16), w_pool_out_ref[...])
    g_b = jax.nn.sigmoid(proj(gate_lo + d, d))
    part = part + g_b * _dot(y_b.astype(BF16), w_conv_out_ref[...])
    partial_ref[...] = part

    def qk_norm(x, g):
        ss = _dot((x * x).astype(BF16), seg_ref[...])
        return (x * lax.rsqrt(ss * (1.0 / ATTN_HEAD_DIM) + EPS) * g).astype(BF16)

    qn = qk_norm(proj(4 * c, c), gq_ref[...])
    kn = qk_norm(proj(5 * c, c), gk_ref[...])
    for hd in range(ATTN_HEADS):
        q_ref[hd] = qn[:, hd * ATTN_V_DIM:(hd + 1) * ATTN_V_DIM]
        k_ref[hd] = kn[:, hd * ATTN_V_DIM:(hd + 1) * ATTN_V_DIM]


def _mixer_in(h, gmix, w_in, w_vt, seg, gq, gk, pool_w, pool_scale, conv_w, w_pool_out,
              w_conv_out, *, layer, tm):
    b, s, d = h.shape
    lr = lambda a: _layer_resident(a.shape, layer)
    c = MIX_WIDTH
    tok = lambda width: pl.BlockSpec((None, tm, width), lambda bi, si: (bi, si, 0))
    head_tok = pl.BlockSpec((None, ATTN_HEADS, tm, ATTN_V_DIM), lambda bi, si: (bi, 0, si, 0))
    return pl.pallas_call(
        functools.partial(_mixer_in_kernel, tm=tm),
        grid=(b, s // tm),
        in_specs=[tok(d), lr(gmix), lr(w_in), lr(w_vt), _resident(seg.shape), lr(gq), lr(gk),
                  lr(pool_w), lr(pool_scale), lr(conv_w), lr(w_pool_out), lr(w_conv_out)],
        out_specs=[tok(d), tok(d), head_tok, head_tok,
                   pl.BlockSpec((None, c, tm), lambda bi, si: (bi, 0, si))],
        out_shape=[jax.ShapeDtypeStruct((b, s, d), F32),
                   jax.ShapeDtypeStruct((b, s, d), F32),
                   jax.ShapeDtypeStruct((b, ATTN_HEADS, s, ATTN_V_DIM), BF16),
                   jax.ShapeDtypeStruct((b, ATTN_HEADS, s, ATTN_V_DIM), BF16),
                   jax.ShapeDtypeStruct((b, c, s), BF16)],
        scratch_shapes=[pltpu.VMEM((POOL_HALO + tm, c), F32),
                        pltpu.VMEM((CONV_HALO + tm, c), F32)],
        compiler_params=pltpu.CompilerParams(
            dimension_semantics=("arbitrary", "arbitrary"),
            vmem_limit_bytes=VMEM_LIMIT_BYTES),
        name="mixer_in",
    )(h, gmix, w_in, w_vt, seg, gq, gk, pool_w, pool_scale, conv_w, w_pool_out, w_conv_out)


def _diff_attn_kernel(bound_ref, q_ref, k_ref, vt_ref, lam_ref, gsub_ref, o_ref,
                      pa0_ref, pa1_ref, pb0_ref, pb1_ref, acc_ref, l_ref, *, layer, tq, lam_init):
    qi = pl.program_id(2)
    q = q_ref[...]
    lane = lax.broadcasted_iota(jnp.int32, q.shape, 1)
    zero = jnp.zeros_like(q)
    q_halves = (jnp.where(lane < ATTN_HEAD_DIM, q, zero), jnp.where(lane >= ATTN_HEAD_DIM, q, zero))
    bound = bound_ref[layer]
    diag = qi * tq
    n_groups = tq // QUERY_GROUP
    acc_ref[...] = jnp.zeros_like(acc_ref)
    l_ref[...] = jnp.zeros_like(l_ref)

    def chunk_mask(shape, key_chunk_offset):
        key_chunk = lax.broadcasted_iota(jnp.int32, shape, 0) // CHUNK + key_chunk_offset
        q_chunk = lax.broadcasted_iota(jnp.int32, shape, 1) // CHUNK
        return key_chunk <= q_chunk

    @pl.when(bound <= MAX_STATIC_SHIFT)
    def _():
        tk = KV_TILE
        set_a, set_b = (pa0_ref, pa1_ref), (pb0_ref, pb1_ref)
        pieces = [(c, g) for c in range(2) for g in range(n_groups)]

        def stage(probs=None, acc=None):
            if probs is not None:
                k = k_ref[pl.ds(pl.multiple_of(probs[0], tk), tk), :]
            if acc is not None:
                vt = vt_ref[:, pl.ds(pl.multiple_of(acc[0], tk), tk)]
            for c, g in pieces:
                cols = slice(g * QUERY_GROUP, (g + 1) * QUERY_GROUP)
                if probs is not None and (probs[2] is None or g >= probs[2]):
                    s = _dot_nt(k, q_halves[c][cols, :])
                    p = jnp.exp2(s - bound)
                    if probs[2] == g:
                        p = jnp.where(chunk_mask(s.shape, 0), p, 0.0)
                    l_ref[c, :, cols] += jnp.sum(p.reshape(tk // 8, 8, QUERY_GROUP), axis=0)
                    probs[1][c, :, cols] = p.astype(BF16)
                if acc is not None and (acc[2] is None or g >= acc[2]):
                    acc_ref[c, :, cols] += _dot(vt, acc[1][c, :, cols])

        stage(probs=(diag, set_b[0], 0))
        stage(probs=(diag + tk, set_b[1], 1))
        for c in range(2):
            set_b[1][c, :, 0:QUERY_GROUP] = jnp.zeros((tk, QUERY_GROUP), BF16)

        def quad(r, carry):
            base = 4 * r * tk
            first = r == 0
            stage(probs=(base, set_a[0], None),
                  acc=(jnp.where(first, diag, base - 2 * tk), set_b[0], None))
            stage(probs=(base + tk, set_a[1], None),
                  acc=(jnp.where(first, diag + tk, base - tk), set_b[1], None))
            stage(probs=(base + 2 * tk, set_b[0], None), acc=(base, set_a[0], None))
            stage(probs=(base + 3 * tk, set_b[1], None), acc=(base + tk, set_a[1], None))
            return carry

        n_diag = tq // tk
        lax.fori_loop(0, qi * n_diag // 4, quad, 0)
        none_below = qi == 0
        pending = [(jnp.where(none_below, diag, diag - 2 * tk), set_b[0], None),
                   (jnp.where(none_below, diag + tk, diag - tk), set_b[1], None)]
        free, busy = set_a, set_b
        for j in range(2, n_diag, 2):
            filled = [(diag + (j + i) * tk, free[i], j + i) for i in range(2)]
            for i in range(2):
                stage(probs=filled[i], acc=pending[i])
            pending, free, busy = filled, busy, free
        for i in range(2):
            stage(acc=pending[i])

    @pl.when(bound > MAX_STATIC_SHIFT)
    def _():
        tg = GENERAL_KV_TILE

        def tile(start, carry, key_chunk_offset=None):
            start = pl.multiple_of(start, tg)
            k = k_ref[pl.ds(start, tg), :]
            vt = vt_ref[:, pl.ds(start, tg)]
            out = []
            for c in range(2):
                m, l = carry[2 * c], carry[2 * c + 1]
                s = _dot_nt(k, q_halves[c])
                if key_chunk_offset is not None:
                    s = jnp.where(chunk_mask(s.shape, key_chunk_offset), s, NEG_INF)
                m_new = jnp.maximum(m, jnp.max(s, axis=0, keepdims=True))
                alpha = jnp.exp2(m - m_new)
                p = jnp.exp2(s - m_new)
                acc_ref[c] = alpha * acc_ref[c] + _dot(vt, p.astype(BF16))
                out += [m_new, alpha * l + jnp.sum(p, axis=0, keepdims=True)]
            return tuple(out)

        init_m = jnp.full((1, tq), NEG_INF, F32)
        init_l = jnp.zeros((1, tq), F32)
        carry = lax.fori_loop(0, qi * (tq // tg), lambda j, cr: tile(j * tg, cr),
                              (init_m, init_l, init_m, init_l))
        for j in range(tq // tg):
            carry = tile(diag + j * tg, carry, key_chunk_offset=j * (tg // CHUNK))
        l_ref[0, 0:1, :] = carry[1]
        l_ref[1, 0:1, :] = carry[3]

    lam = (jnp.exp(jnp.sum(lam_ref[0:1, :] * lam_ref[1:2, :], axis=-1, keepdims=True))
           - jnp.exp(jnp.sum(lam_ref[2:3, :] * lam_ref[3:4, :], axis=-1, keepdims=True))
           + lam_init)
    l1 = jnp.sum(l_ref[0], axis=0, keepdims=True)
    l2 = jnp.sum(l_ref[1], axis=0, keepdims=True)
    o = acc_ref[0] / l1 - lam * (acc_ref[1] / l2)
    ms = jnp.mean(o * o, axis=0, keepdims=True)
    o = o * lax.rsqrt(ms + EPS)
    o_ref[...] = (o.T * gsub_ref[...] * (1.0 - lam_init)).astype(o_ref.dtype)


def _diff_attn(bound, q, k, vt, lam_vecs, gsub, *, layer, tq, lam_init):
    b, nh, s, dv = q.shape
    assert tq % (4 * KV_TILE) == 0 and tq % GENERAL_KV_TILE == 0 and s % tq == 0
    p_buffer = pltpu.VMEM((2, KV_TILE, tq), BF16)
    return pl.pallas_call(
        functools.partial(_diff_attn_kernel, layer=layer, tq=tq, lam_init=lam_init),
        grid=(b, nh, s // tq),
        in_specs=[pl.BlockSpec(memory_space=pltpu.SMEM),
                  pl.BlockSpec((None, None, tq, dv), lambda bi, hi, qi: (bi, hi, qi, 0)),
                  pl.BlockSpec((None, None, s, dv), lambda bi, hi, qi: (bi, hi, 0, 0)),
                  pl.BlockSpec((None, dv, s), lambda bi, hi, qi: (bi, hi, 0)),
                  _layer_resident(lam_vecs.shape, layer), _layer_resident(gsub.shape, layer)],
        out_specs=pl.BlockSpec((None, tq, dv), lambda bi, hi, qi: (bi, qi, hi)),
        out_shape=jax.ShapeDtypeStruct((b, s, nh * dv), BF16),
        scratch_shapes=[p_buffer, p_buffer, p_buffer, p_buffer,
                        pltpu.VMEM((2, dv, tq), F32), pltpu.VMEM((2, 8, tq), F32)],
        compiler_params=pltpu.CompilerParams(
            dimension_semantics=("arbitrary", "arbitrary", "arbitrary"),
            vmem_limit_bytes=VMEM_LIMIT_BYTES),
        name="diff_attn",
    )(bound, q, k, vt, lam_vecs, gsub)


def _mixer_out_kernel(h_ref, partial_ref, gc_ref, yc_ref, p_ref, w_attn_out_ref, w_o_ref,
                      gmlp_ref, w_up_ref, w_down_ref, gple_ref, w_gate_ref, w_proj_ref,
                      out_ref, *, ff_chunk):
    merged = partial_ref[...] + gc_ref[...] * _dot(yc_ref[...], w_attn_out_ref[...])
    h1 = h_ref[...] + _dot(merged.astype(BF16), w_o_ref[...])
    embed = _dot(p_ref[...].astype(BF16), w_proj_ref[...])

    m = _rms_rows(h1, gmlp_ref[...]).astype(BF16)
    d_ff = w_up_ref.shape[1]
    mlp = jnp.zeros_like(h1)
    for lo in range(0, d_ff, ff_chunk):
        up = _dot(m, w_up_ref[:, lo:lo + ff_chunk])
        act = jnp.square(jnp.maximum(up, 0.0)).astype(BF16)
        mlp = mlp + _dot(act, w_down_ref[lo:lo + ff_chunk, :])
    h2 = h1 + mlp

    e = _rms_rows(h2, gple_ref[...]).astype(BF16)
    gate = jax.nn.sigmoid(_dot(e, w_gate_ref[...]))
    out_ref[...] = h2 + gate * embed


def _mixer_out(h, partial, gc, yc, p, w_attn_out, w_o, gmlp, w_up, w_down, gple, w_gate, w_proj,
               *, layer, tm, ff_chunk):
    b, s, d = h.shape
    lr = lambda a: _layer_resident(a.shape, layer)
    tok = lambda width: pl.BlockSpec((None, tm, width), lambda bi, si: (bi, si, 0))
    p_tok = pl.BlockSpec((None, None, tm, p.shape[-1]), lambda bi, si: (layer, bi, si, 0))
    return pl.pallas_call(
        functools.partial(_mixer_out_kernel, ff_chunk=ff_chunk),
        grid=(b, s // tm),
        in_specs=[tok(d), tok(d), tok(d), tok(yc.shape[-1]), p_tok,
                  lr(w_attn_out), lr(w_o), lr(gmlp), lr(w_up), lr(w_down), lr(gple), lr(w_gate),
                  lr(w_proj)],
        out_specs=tok(d),
        out_shape=jax.ShapeDtypeStruct((b, s, d), F32),
        compiler_params=pltpu.CompilerParams(
            dimension_semantics=("arbitrary", "arbitrary"),
            vmem_limit_bytes=VMEM_LIMIT_BYTES),
        name="mixer_out",
    )(h, partial, gc, yc, p, w_attn_out, w_o, gmlp, w_up, w_down, gple, w_gate, w_proj)


def _block_diag(blocks):
    g, n, _ = blocks.shape
    eye = jnp.eye(g, dtype=blocks.dtype)
    return (eye[:, None, :, None] * blocks[:, :, None, :]).reshape(g * n, g * n)


def kernel(x, p, norm_mix_g, w_in, pool_w, pool_scale, conv_w, q_norm_g, k_norm_g, lam_q1, lam_k1,
           lam_q2, lam_k2, sub_norm_g, w_pool_out, w_conv_out, w_attn_out, w_o, norm_mlp_g, w_up,
           w_down, norm_ple_g, w_ple_gate, w_ple_proj):
    depth = w_in.shape[0]
    c = MIX_WIDTH
    tm_in, tm_out, tq, ff_chunk = 512, 512, 2048, 1024
    n_seg = c // ATTN_HEAD_DIM
    seg = _block_diag(jnp.ones((n_seg, ATTN_HEAD_DIM, ATTN_HEAD_DIM), BF16))
    rows = lambda v: v.reshape(depth, 1, -1).astype(F32)
    bf = lambda w: w.astype(BF16)
    score_scale = ATTN_HEAD_DIM ** -0.5 * math.log2(math.e)

    w_in_b = bf(w_in)
    w_vt = bf(jnp.swapaxes(w_in[:, :, 6 * c:7 * c], 1, 2))
    gq = rows(jnp.tile(q_norm_g, (1, n_seg)) * score_scale)
    gk = rows(jnp.tile(k_norm_g, (1, n_seg)))
    pool_bd = bf(jax.vmap(_block_diag)(pool_w))
    lam_vecs = jnp.stack([lam_q1, lam_k1, lam_q2, lam_k2], axis=1).astype(F32)
    score_bound = (64.0 * SCORE_BOUND_MARGIN * score_scale * jnp.max(jnp.abs(q_norm_g), axis=-1)
                   * jnp.max(jnp.abs(k_norm_g), axis=-1)).astype(F32)
    mixer_in_params = (rows(norm_mix_g), w_in_b, w_vt, seg, gq, gk, pool_bd, rows(pool_scale),
                       conv_w.astype(F32), bf(w_pool_out), bf(w_conv_out))
    mixer_out_params = (bf(w_attn_out), bf(w_o), rows(norm_mlp_g), bf(w_up), bf(w_down),
                        rows(norm_ple_g), bf(w_ple_gate), bf(w_ple_proj))
    gsub = rows(sub_norm_g)

    h = x
    for i in range(depth):
        lam_init = 0.8 - 0.6 * math.exp(-0.3 * i)
        partial, gc, q, k, vt = _mixer_in(h, *mixer_in_params, layer=i, tm=tm_in)
        yc = _diff_attn(score_bound, q, k, vt, lam_vecs, gsub, layer=i, tq=tq, lam_init=lam_init)
        h = _mixer_out(h, partial, gc, yc, p, *mixer_out_params, layer=i, tm=tm_out,
                       ff_chunk=ff_chunk)
    return h
```

```python
import functools
import math

import jax
import jax.numpy as jnp
from jax import lax
from jax.experimental import pallas as pl
from jax.experimental.pallas import tpu as pltpu

MIX_WIDTH = 512
POOL_WINDOWS = (2, 4, 8, 16)
POOL_GROUP_DIM = MIX_WIDTH // len(POOL_WINDOWS)
CONV_WIDTH = 3
ATTN_HEADS = 4
ATTN_HEAD_DIM = 64
ATTN_V_DIM = 2 * ATTN_HEAD_DIM
CHUNK = 64
QUERY_GROUP = 256
KV_TILE = 256
GENERAL_KV_TILE = 256
MAX_STATIC_SHIFT = 30.0
SCORE_BOUND_MARGIN = 1.02
EPS = 1e-6
NEG_INF = -1e30

POOL_HALO = 16
CONV_HALO = 8
VMEM_LIMIT_BYTES = 56 * 1024 * 1024

F32 = jnp.float32
BF16 = jnp.bfloat16


def _dot(a, b):
    return jnp.dot(a, b, preferred_element_type=F32)


def _dot_nt(a, b):
    return lax.dot_general(a, b, (((1,), (1,)), ((), ())), preferred_element_type=F32)


def _rms_rows(x, g):
    ms = jnp.mean(x * x, axis=-1, keepdims=True)
    return x * lax.rsqrt(ms + EPS) * g


def _resident(shape):
    nd = len(shape)
    return pl.BlockSpec(shape, lambda *_: (0,) * nd, pipeline_mode=pl.Buffered(1))


def _layer_resident(stacked_shape, layer):
    nd = len(stacked_shape)
    return pl.BlockSpec((None,) + tuple(stacked_shape[1:]), lambda *_: (layer,) + (0,) * (nd - 1),
                        pipeline_mode=pl.Buffered(1))


def _mixer_in_kernel(h_ref, gmix_ref, w_in_ref, seg_ref, gq_ref, gk_ref,
                     pool_w_ref, pool_scale_ref, conv_w_ref, w_pool_out_ref, w_conv_out_ref,
                     partial_ref, gc_ref, q_ref, k_ref, vt_ref,
                     abuf_ref, zbuf_ref, *, tm):
    s_idx = pl.program_id(1)
    c = MIX_WIDTH

    @pl.when(s_idx == 0)
    def _():
        abuf_ref[0:POOL_HALO, :] = jnp.zeros((POOL_HALO, c), F32)
        zbuf_ref[0:CONV_HALO, :] = jnp.zeros((CONV_HALO, c), F32)

    u = _rms_rows(h_ref[...], gmix_ref[...]).astype(BF16)

    def proj(lo, width):
        return _dot(u, w_in_ref[:, lo:lo + width])

    gate_lo = 7 * c
    d = partial_ref.shape[-1]
    a = proj(0, c)
    abuf_ref[POOL_HALO:POOL_HALO + tm, :] = a
    z = proj(3 * c, c) * proj(c, c)
    zbuf_ref[CONV_HALO:CONV_HALO + tm, :] = z
    gc_ref[...] = jax.nn.sigmoid(proj(gate_lo + 2 * d, d))

    t_pos = (s_idx * tm + lax.broadcasted_iota(jnp.int32, (tm, POOL_GROUP_DIM), 0) + 1).astype(F32)
    pooled = []
    for g, win in enumerate(POOL_WINDOWS):
        lo = g * POOL_GROUP_DIM
        wsum = a[:, lo:lo + POOL_GROUP_DIM]
        for back in range(1, win):
            wsum = wsum + abuf_ref[POOL_HALO - back:POOL_HALO - back + tm, lo:lo + POOL_GROUP_DIM]
        cnt = jnp.minimum(t_pos, float(win))
        pooled.append(wsum / cnt - a[:, lo:lo + POOL_GROUP_DIM])
    pooled = jnp.concatenate(pooled, axis=-1).astype(BF16)
    y_a = _dot(pooled, pool_w_ref[...]) * pool_scale_ref[...]
    abuf_ref[0:POOL_HALO, :] = abuf_ref[tm:tm + POOL_HALO, :]

    vt_ref[...] = proj(6 * c, c).T.astype(BF16)

    conv = conv_w_ref[CONV_WIDTH - 1:CONV_WIDTH, :] * z
    for back in range(1, CONV_WIDTH):
        tap = conv_w_ref[CONV_WIDTH - 1 - back:CONV_WIDTH - back, :]
        conv = conv + tap * zbuf_ref[CONV_HALO - back:CONV_HALO - back + tm, :]
    y_b = proj(2 * c, c) * conv
    zbuf_ref[0:CONV_HALO, :] = zbuf_ref[tm:tm + CONV_HALO, :]

    g_a = jax.nn.sigmoid(proj(gate_lo, d))
    part = g_a * _dot(y_a.astype(BF16), w_pool_out_ref[...])
    g_b = jax.nn.sigmoid(proj(gate_lo + d, d))
    part = part + g_b * _dot(y_b.astype(BF16), w_conv_out_ref[...])
    partial_ref[...] = part

    def qk_norm(x, g):
        ss = _dot((x * x).astype(BF16), seg_ref[...])
        return (x * lax.rsqrt(ss * (1.0 / ATTN_HEAD_DIM) + EPS) * g).astype(BF16)

    qn = qk_norm(proj(4 * c, c), gq_ref[...])
    kn = qk_norm(proj(5 * c, c), gk_ref[...])
    for hd in range(ATTN_HEADS):
        q_ref[hd] = qn[:, hd * ATTN_V_DIM:(hd + 1) * ATTN_V_DIM]
        k_ref[hd] = kn[:, hd * ATTN_V_DIM:(hd + 1) * ATTN_V_DIM]


def _mixer_in(h, gmix, w_in, seg, gq, gk, pool_w, pool_scale, conv_w, w_pool_out,
              w_conv_out, *, layer, tm):
    b, s, d = h.shape
    lr = lambda a: _layer_resident(a.shape, layer)
    c = MIX_WIDTH
    tok = lambda width: pl.BlockSpec((None, tm, width), lambda bi, si: (bi, si, 0))
    head_tok = pl.BlockSpec((None, ATTN_HEADS, tm, ATTN_V_DIM), lambda bi, si: (bi, 0, si, 0))
    return pl.pallas_call(
        functools.partial(_mixer_in_kernel, tm=tm),
        grid=(b, s // tm),
        in_specs=[tok(d), lr(gmix), lr(w_in), _resident(seg.shape), lr(gq), lr(gk),
                  lr(pool_w), lr(pool_scale), lr(conv_w), lr(w_pool_out), lr(w_conv_out)],
        out_specs=[tok(d), tok(d), head_tok, head_tok,
                   pl.BlockSpec((None, c, tm), lambda bi, si: (bi, 0, si))],
        out_shape=[jax.ShapeDtypeStruct((b, s, d), F32),
                   jax.ShapeDtypeStruct((b, s, d), F32),
                   jax.ShapeDtypeStruct((b, ATTN_HEADS, s, ATTN_V_DIM), BF16),
                   jax.ShapeDtypeStruct((b, ATTN_HEADS, s, ATTN_V_DIM), BF16),
                   jax.ShapeDtypeStruct((b, c, s), BF16)],
        scratch_shapes=[pltpu.VMEM((POOL_HALO + tm, c), F32),
                        pltpu.VMEM((CONV_HALO + tm, c), F32)],
        compiler_params=pltpu.CompilerParams(
            dimension_semantics=("arbitrary", "arbitrary"),
            vmem_limit_bytes=VMEM_LIMIT_BYTES),
        name="mixer_in",
    )(h, gmix, w_in, seg, gq, gk, pool_w, pool_scale, conv_w, w_pool_out, w_conv_out)


def _diff_attn_kernel(bound_ref, q_ref, k_ref, vt_ref, lam_ref, gsub_ref, o_ref,
                      pa0_ref, pa1_ref, pb0_ref, pb1_ref, acc_ref, l_ref, *, layer, tq, lam_init):
    qi = pl.program_id(2)
    q = q_ref[...]
    lane = lax.broadcasted_iota(jnp.int32, q.shape, 1)
    zero = jnp.zeros_like(q)
    q_halves = (jnp.where(lane < ATTN_HEAD_DIM, q, zero), jnp.where(lane >= ATTN_HEAD_DIM, q, zero))
    bound = bound_ref[layer]
    diag = qi * tq
    n_groups = tq // QUERY_GROUP
    acc_ref[...] = jnp.zeros_like(acc_ref)
    l_ref[...] = jnp.zeros_like(l_ref)

    def chunk_mask(shape, key_chunk_offset):
        key_chunk = lax.broadcasted_iota(jnp.int32, shape, 0) // CHUNK + key_chunk_offset
        q_chunk = lax.broadcasted_iota(jnp.int32, shape, 1) // CHUNK
        return key_chunk <= q_chunk

    @pl.when(bound <= MAX_STATIC_SHIFT)
    def _():
        tk = KV_TILE
        set_a, set_b = (pa0_ref, pa1_ref), (pb0_ref, pb1_ref)
        pieces = [(c, g) for c in range(2) for g in range(n_groups)]

        def stage(probs=None, acc=None):
            if probs is not None:
                k = k_ref[pl.ds(pl.multiple_of(probs[0], tk), tk), :]
            if acc is not None:
                vt = vt_ref[:, pl.ds(pl.multiple_of(acc[0], tk), tk)]
            for c, g in pieces:
                cols = slice(g * QUERY_GROUP, (g + 1) * QUERY_GROUP)
                if probs is not None and (probs[2] is None or g >= probs[2]):
                    s = _dot_nt(k, q_halves[c][cols, :])
                    p = jnp.exp2(s - bound)
                    if probs[2] == g:
                        p = jnp.where(chunk_mask(s.shape, 0), p, 0.0)
                    l_ref[c, :, cols] += jnp.sum(p.reshape(tk // 8, 8, QUERY_GROUP), axis=0)
                    probs[1][c, :, cols] = p.astype(BF16)
                if acc is not None and (acc[2] is None or g >= acc[2]):
                    acc_ref[c, :, cols] += _dot(vt, acc[1][c, :, cols])

        stage(probs=(diag, set_b[0], 0))
        stage(probs=(diag + tk, set_b[1], 1))
        for c in range(2):
            set_b[1][c, :, 0:QUERY_GROUP] = jnp.zeros((tk, QUERY_GROUP), BF16)

        def quad(r, carry):
            base = 4 * r * tk
            first = r == 0
            stage(probs=(base, set_a[0], None),
                  acc=(jnp.where(first, diag, base - 2 * tk), set_b[0], None))
            stage(probs=(base + tk, set_a[1], None),
                  acc=(jnp.where(first, diag + tk, base - tk), set_b[1], None))
            stage(probs=(base + 2 * tk, set_b[0], None), acc=(base, set_a[0], None))
            stage(probs=(base + 3 * tk, set_b[1], None), acc=(base + tk, set_a[1], None))
            return carry

        n_diag = tq // tk
        lax.fori_loop(0, qi * n_diag // 4, quad, 0)
        none_below = qi == 0
        pending = [(jnp.where(none_below, diag, diag - 2 * tk), set_b[0], None),
                   (jnp.where(none_below, diag + tk, diag - tk), set_b[1], None)]
        free, busy = set_a, set_b
        for j in range(2, n_diag, 2):
            filled = [(diag + (j + i) * tk, free[i], j + i) for i in range(2)]
            for i in range(2):
                stage(probs=filled[i], acc=pending[i])
            pending, free, busy = filled, busy, free
        for i in range(2):
            stage(acc=pending[i])

    @pl.when(bound > MAX_STATIC_SHIFT)
    def _():
        tg = GENERAL_KV_TILE

        def tile(start, carry, key_chunk_offset=None):
            start = pl.multiple_of(start, tg)
            k = k_ref[pl.ds(start, tg), :]
            vt = vt_ref[:, pl.ds(start, tg)]
            out = []
            for c in range(2):
                m, l = carry[2 * c], carry[2 * c + 1]
                s = _dot_nt(k, q_halves[c])
                if key_chunk_offset is not None:
                    s = jnp.where(chunk_mask(s.shape, key_chunk_offset), s, NEG_INF)
                m_new = jnp.maximum(m, jnp.max(s, axis=0, keepdims=True))
                alpha = jnp.exp2(m - m_new)
                p = jnp.exp2(s - m_new)
                acc_ref[c] = alpha * acc_ref[c] + _dot(vt, p.astype(BF16))
                out += [m_new, alpha * l + jnp.sum(p, axis=0, keepdims=True)]
            return tuple(out)

        init_m = jnp.full((1, tq), NEG_INF, F32)
        init_l = jnp.zeros((1, tq), F32)
        carry = lax.fori_loop(0, qi * (tq // tg), lambda j, cr: tile(j * tg, cr),
                              (init_m, init_l, init_m, init_l))
        for j in range(tq // tg):
            carry = tile(diag + j * tg, carry, key_chunk_offset=j * (tg // CHUNK))
        l_ref[0, 0:1, :] = carry[1]
        l_ref[1, 0:1, :] = carry[3]

    lam = (jnp.exp(jnp.sum(lam_ref[0:1, :] * lam_ref[1:2, :], axis=-1, keepdims=True))
           - jnp.exp(jnp.sum(lam_ref[2:3, :] * lam_ref[3:4, :], axis=-1, keepdims=True))
           + lam_init)
    l1 = jnp.sum(l_ref[0], axis=0, keepdims=True)
    l2 = jnp.sum(l_ref[1], axis=0, keepdims=True)
    o = acc_ref[0] / l1 - lam * (acc_ref[1] / l2)
    ms = jnp.mean(o * o, axis=0, keepdims=True)
    o = o * lax.rsqrt(ms + EPS)
    o_ref[...] = (o.T * gsub_ref[...] * (1.0 - lam_init)).astype(o_ref.dtype)


def _diff_attn(bound, q, k, vt, lam_vecs, gsub, *, layer, tq, lam_init):
    b, nh, s, dv = q.shape
    assert tq % (4 * KV_TILE) == 0 and tq % GENERAL_KV_TILE == 0 and s % tq == 0
    p_buffer = pltpu.VMEM((2, KV_TILE, tq), BF16)
    return pl.pallas_call(
        functools.partial(_diff_attn_kernel, layer=layer, tq=tq, lam_init=lam_init),
        grid=(b, nh, s // tq),
        in_specs=[pl.BlockSpec(memory_space=pltpu.SMEM),
                  pl.BlockSpec((None, None, tq, dv), lambda bi, hi, qi: (bi, hi, qi, 0)),
                  pl.BlockSpec((None, None, s, dv), lambda bi, hi, qi: (bi, hi, 0, 0)),
                  pl.BlockSpec((None, dv, s), lambda bi, hi, qi: (bi, hi, 0)),
                  _layer_resident(lam_vecs.shape, layer), _layer_resident(gsub.shape, layer)],
        out_specs=pl.BlockSpec((None, tq, dv), lambda bi, hi, qi: (bi, qi, hi)),
        out_shape=jax.ShapeDtypeStruct((b, s, nh * dv), BF16),
        scratch_shapes=[p_buffer, p_buffer, p_buffer, p_buffer,
                        pltpu.VMEM((2, dv, tq), F32), pltpu.VMEM((2, 8, tq), F32)],
        compiler_params=pltpu.CompilerParams(
            dimension_semantics=("arbitrary", "arbitrary", "arbitrary"),
            vmem_limit_bytes=VMEM_LIMIT_BYTES),
        name="diff_attn",
    )(bound, q, k, vt, lam_vecs, gsub)


def _mixer_out_kernel(h_ref, partial_ref, gc_ref, yc_ref, p_ref, w_attn_out_ref, w_o_ref,
                      gmlp_ref, w_up_ref, w_down_ref, gple_ref, w_gate_ref, w_proj_ref,
                      out_ref, *, ff_chunk):
    merged = partial_ref[...] + gc_ref[...] * _dot(yc_ref[...], w_attn_out_ref[...])
    h1 = h_ref[...] + _dot(merged.astype(BF16), w_o_ref[...])
    embed = _dot(p_ref[...].astype(BF16), w_proj_ref[...])

    m = _rms_rows(h1, gmlp_ref[...]).astype(BF16)
    d_ff = w_up_ref.shape[1]
    mlp = jnp.zeros_like(h1)
    for lo in range(0, d_ff, ff_chunk):
        up = _dot(m, w_up_ref[:, lo:lo + ff_chunk])
        act = jnp.square(jnp.maximum(up, 0.0)).astype(BF16)
        mlp = mlp + _dot(act, w_down_ref[lo:lo + ff_chunk, :])
    h2 = h1 + mlp

    e = _rms_rows(h2, gple_ref[...]).astype(BF16)
    gate = jax.nn.sigmoid(_dot(e, w_gate_ref[...]))
    out_ref[...] = h2 + gate * embed


def _mixer_out(h, partial, gc, yc, p, w_attn_out, w_o, gmlp, w_up, w_down, gple, w_gate, w_proj,
               *, layer, tm, ff_chunk):
    b, s, d = h.shape
    lr = lambda a: _layer_resident(a.shape, layer)
    tok = lambda width: pl.BlockSpec((None, tm, width), lambda bi, si: (bi, si, 0))
    p_tok = pl.BlockSpec((None, None, tm, p.shape[-1]), lambda bi, si: (layer, bi, si, 0))
    return pl.pallas_call(
        functools.partial(_mixer_out_kernel, ff_chunk=ff_chunk),
        grid=(b, s // tm),
        in_specs=[tok(d), tok(d), tok(d), tok(yc.shape[-1]), p_tok,
                  lr(w_attn_out), lr(w_o), lr(gmlp), lr(w_up), lr(w_down), lr(gple), lr(w_gate),
                  lr(w_proj)],
        out_specs=tok(d),
        out_shape=jax.ShapeDtypeStruct((b, s, d), F32),
        compiler_params=pltpu.CompilerParams(
            dimension_semantics=("arbitrary", "arbitrary"),
            vmem_limit_bytes=VMEM_LIMIT_BYTES),
        name="mixer_out",
    )(h, partial, gc, yc, p, w_attn_out, w_o, gmlp, w_up, w_down, gple, w_gate, w_proj)


def _block_diag(blocks):
    g, n, _ = blocks.shape
    eye = jnp.eye(g, dtype=blocks.dtype)
    return (eye[:, None, :, None] * blocks[:, :, None, :]).reshape(g * n, g * n)


def kernel(x, p, norm_mix_g, w_in, pool_w, pool_scale, conv_w, q_norm_g, k_norm_g, lam_q1, lam_k1,
           lam_q2, lam_k2, sub_norm_g, w_pool_out, w_conv_out, w_attn_out, w_o, norm_mlp_g, w_up,
           w_down, norm_ple_g, w_ple_gate, w_ple_proj):
    depth = w_in.shape[0]
    c = MIX_WIDTH
    tm_in, tm_out, tq, ff_chunk = 512, 512, 2048, 1024
    n_seg = c // ATTN_HEAD_DIM
    seg = _block_diag(jnp.ones((n_seg, ATTN_HEAD_DIM, ATTN_HEAD_DIM), BF16))
    rows = lambda v: v.reshape(depth, 1, -1).astype(F32)
    bf = lambda w: w.astype(BF16)
    score_scale = ATTN_HEAD_DIM ** -0.5 * math.log2(math.e)

    gq = rows(jnp.tile(q_norm_g, (1, n_seg)) * score_scale)
    gk = rows(jnp.tile(k_norm_g, (1, n_seg)))
    pool_bd = bf(jax.vmap(_block_diag)(pool_w))
    lam_vecs = jnp.stack([lam_q1, lam_k1, lam_q2, lam_k2], axis=1).astype(F32)
    score_bound = (64.0 * SCORE_BOUND_MARGIN * score_scale * jnp.max(jnp.abs(q_norm_g), axis=-1)
                   * jnp.max(jnp.abs(k_norm_g), axis=-1)).astype(F32)
    mixer_in_params = (rows(norm_mix_g), bf(w_in), seg, gq, gk, pool_bd, rows(pool_scale),
                       conv_w.astype(F32), bf(w_pool_out), bf(w_conv_out))
    mixer_out_params = (bf(w_attn_out), bf(w_o), rows(norm_mlp_g), bf(w_up), bf(w_down),
                        rows(norm_ple_g), bf(w_ple_gate), bf(w_ple_proj))
    gsub = rows(sub_norm_g)

    h = x
    for i in range(depth):
        lam_init = 0.8 - 0.6 * math.exp(-0.3 * i)
        partial, gc, q, k, vt = _mixer_in(h, *mixer_in_params, layer=i, tm=tm_in)
        yc = _diff_attn(score_bound, q, k, vt, lam_vecs, gsub, layer=i, tq=tq, lam_init=lam_init)
        h = _mixer_out(h, partial, gc, yc, p, *mixer_out_params, layer=i, tm=tm_out,
                       ff_chunk=ff_chunk)
    return h
```

```python
import functools
import math

import jax
import jax.numpy as jnp
from jax import lax
from jax.experimental import pallas as pl
from jax.experimental.pallas import tpu as pltpu

MIX_WIDTH = 512
POOL_WINDOWS = (2, 4, 8, 16)
POOL_GROUP_DIM = MIX_WIDTH // len(POOL_WINDOWS)
CONV_WIDTH = 3
ATTN_HEADS = 4
ATTN_HEAD_DIM = 64
ATTN_V_DIM = 2 * ATTN_HEAD_DIM
CHUNK = 64
MXU_TILE = 256
QUERY_GROUP = MXU_TILE
KV_TILE = 256
GENERAL_KV_TILE = 256
MAX_STATIC_SHIFT = 30.0
SCORE_BOUND_MARGIN = 1.02
EPS = 1e-6
NEG_INF = -1e30

POOL_HALO = 16
CONV_HALO = 8
VMEM_LIMIT_BYTES = 56 * 1024 * 1024

F32 = jnp.float32
BF16 = jnp.bfloat16


def _dot(a, b):
    return jnp.dot(a, b, preferred_element_type=F32)


def _dot_nt(a, b):
    return lax.dot_general(a, b, (((1,), (1,)), ((), ())), preferred_element_type=F32)


def _block_diag_dot(x, w_ref):
    n = w_ref.shape[0]
    tiles = [slice(lo, lo + MXU_TILE) for lo in range(0, n, MXU_TILE)]
    return jnp.concatenate([_dot(x[:, t], w_ref[t, t]) for t in tiles], axis=-1)


def _rms_rows(x, g):
    ms = jnp.mean(x * x, axis=-1, keepdims=True)
    return x * lax.rsqrt(ms + EPS) * g


def _resident(shape):
    nd = len(shape)
    return pl.BlockSpec(shape, lambda *_: (0,) * nd, pipeline_mode=pl.Buffered(1))


def _layer_resident(stacked_shape, layer):
    nd = len(stacked_shape)
    return pl.BlockSpec((None,) + tuple(stacked_shape[1:]), lambda *_: (layer,) + (0,) * (nd - 1),
                        pipeline_mode=pl.Buffered(1))


def _mixer_in_kernel(h_ref, gmix_ref, w_in_ref, seg_ref, gq_ref, gk_ref,
                     pool_w_ref, pool_scale_ref, conv_w_ref, w_pool_out_ref, w_conv_out_ref,
                     partial_ref, gc_ref, q_ref, k_ref, vt_ref,
                     abuf_ref, zbuf_ref, vbuf_ref, *, tm):
    s_idx = pl.program_id(1)
    c = MIX_WIDTH

    @pl.when(s_idx == 0)
    def _():
        abuf_ref[0:POOL_HALO, :] = jnp.zeros((POOL_HALO, c), F32)
        zbuf_ref[0:CONV_HALO, :] = jnp.zeros((CONV_HALO, c), F32)

    u = _rms_rows(h_ref[...], gmix_ref[...]).astype(BF16)

    def proj(lo, width):
        return _dot(u, w_in_ref[:, lo:lo + width])

    gate_lo = 7 * c
    d = partial_ref.shape[-1]
    a = proj(0, c)
    abuf_ref[POOL_HALO:POOL_HALO + tm, :] = a
    z = proj(3 * c, c) * proj(c, c)
    zbuf_ref[CONV_HALO:CONV_HALO + tm, :] = z
    gc_ref[...] = jax.nn.sigmoid(proj(gate_lo + 2 * d, d))

    t_pos = (s_idx * tm + lax.broadcasted_iota(jnp.int32, (tm, POOL_GROUP_DIM), 0) + 1).astype(F32)
    pooled = []
    for g, win in enumerate(POOL_WINDOWS):
        lo = g * POOL_GROUP_DIM
        wsum = a[:, lo:lo + POOL_GROUP_DIM]
        for back in range(1, win):
            wsum = wsum + abuf_ref[POOL_HALO - back:POOL_HALO - back + tm, lo:lo + POOL_GROUP_DIM]
        cnt = jnp.minimum(t_pos, float(win))
        pooled.append(wsum / cnt - a[:, lo:lo + POOL_GROUP_DIM])
    pooled = jnp.concatenate(pooled, axis=-1).astype(BF16)
    y_a = _block_diag_dot(pooled, pool_w_ref) * pool_scale_ref[...]
    abuf_ref[0:POOL_HALO, :] = abuf_ref[tm:tm + POOL_HALO, :]

    vbuf_ref[...] = proj(6 * c, c)
    vt_ref[...] = vbuf_ref[...].T.astype(BF16)

    conv = conv_w_ref[CONV_WIDTH - 1:CONV_WIDTH, :] * z
    for back in range(1, CONV_WIDTH):
        tap = conv_w_ref[CONV_WIDTH - 1 - back:CONV_WIDTH - back, :]
        conv = conv + tap * zbuf_ref[CONV_HALO - back:CONV_HALO - back + tm, :]
    y_b = proj(2 * c, c) * conv
    zbuf_ref[0:CONV_HALO, :] = zbuf_ref[tm:tm + CONV_HALO, :]

    g_a = jax.nn.sigmoid(proj(gate_lo, d))
    part = g_a * _dot(y_a.astype(BF16), w_pool_out_ref[...])
    g_b = jax.nn.sigmoid(proj(gate_lo + d, d))
    part = part + g_b * _dot(y_b.astype(BF16), w_conv_out_ref[...])
    partial_ref[...] = part

    def qk_norm(x, g):
        ss = _block_diag_dot((x * x).astype(BF16), seg_ref)
        return (x * lax.rsqrt(ss * (1.0 / ATTN_HEAD_DIM) + EPS) * g).astype(BF16)

    qn = qk_norm(proj(4 * c, c), gq_ref[...])
    kn = qk_norm(proj(5 * c, c), gk_ref[...])
    for hd in range(ATTN_HEADS):
        q_ref[hd] = qn[:, hd * ATTN_V_DIM:(hd + 1) * ATTN_V_DIM]
        k_ref[hd] = kn[:, hd * ATTN_V_DIM:(hd + 1) * ATTN_V_DIM]


def _mixer_in(h, gmix, w_in, seg, gq, gk, pool_w, pool_scale, conv_w, w_pool_out,
              w_conv_out, *, layer, tm):
    b, s, d = h.shape
    lr = lambda a: _layer_resident(a.shape, layer)
    c = MIX_WIDTH
    tok = lambda width: pl.BlockSpec((None, tm, width), lambda bi, si: (bi, si, 0))
    head_tok = pl.BlockSpec((None, ATTN_HEADS, tm, ATTN_V_DIM), lambda bi, si: (bi, 0, si, 0))
    return pl.pallas_call(
        functools.partial(_mixer_in_kernel, tm=tm),
        grid=(b, s // tm),
        in_specs=[tok(d), lr(gmix), lr(w_in), _resident(seg.shape), lr(gq), lr(gk),
                  lr(pool_w), lr(pool_scale), lr(conv_w), lr(w_pool_out), lr(w_conv_out)],
        out_specs=[tok(d), tok(d), head_tok, head_tok,
                   pl.BlockSpec((None, c, tm), lambda bi, si: (bi, 0, si))],
        out_shape=[jax.ShapeDtypeStruct((b, s, d), F32),
                   jax.ShapeDtypeStruct((b, s, d), F32),
                   jax.ShapeDtypeStruct((b, ATTN_HEADS, s, ATTN_V_DIM), BF16),
                   jax.ShapeDtypeStruct((b, ATTN_HEADS, s, ATTN_V_DIM), BF16),
                   jax.ShapeDtypeStruct((b, c, s), BF16)],
        scratch_shapes=[pltpu.VMEM((POOL_HALO + tm, c), F32),
                        pltpu.VMEM((CONV_HALO + tm, c), F32),
                        pltpu.VMEM((tm, c), F32)],
        compiler_params=pltpu.CompilerParams(
            dimension_semantics=("arbitrary", "arbitrary"),
            vmem_limit_bytes=VMEM_LIMIT_BYTES),
        name="mixer_in",
    )(h, gmix, w_in, seg, gq, gk, pool_w, pool_scale, conv_w, w_pool_out, w_conv_out)


def _diff_attn_kernel(bound_ref, q_ref, k_ref, vt_ref, lam_ref, gsub_ref, o_ref,
                      pa0_ref, pa1_ref, pb0_ref, pb1_ref, acc_ref, l_ref, *, layer, tq, lam_init):
    qi = pl.program_id(2)
    q = q_ref[...]
    lane = lax.broadcasted_iota(jnp.int32, q.shape, 1)
    zero = jnp.zeros_like(q)
    q_halves = (jnp.where(lane < ATTN_HEAD_DIM, q, zero), jnp.where(lane >= ATTN_HEAD_DIM, q, zero))
    bound = bound_ref[layer]
    diag = qi * tq
    n_groups = tq // QUERY_GROUP
    acc_ref[...] = jnp.zeros_like(acc_ref)
    l_ref[...] = jnp.zeros_like(l_ref)

    def chunk_mask(shape, key_chunk_offset):
        key_chunk = lax.broadcasted_iota(jnp.int32, shape, 0) // CHUNK + key_chunk_offset
        q_chunk = lax.broadcasted_iota(jnp.int32, shape, 1) // CHUNK
        return key_chunk <= q_chunk

    @pl.when(bound <= MAX_STATIC_SHIFT)
    def _():
        tk = KV_TILE
        set_a, set_b = (pa0_ref, pa1_ref), (pb0_ref, pb1_ref)
        pieces = [(c, g) for c in range(2) for g in range(n_groups)]

        def stage(probs=None, acc=None):
            if probs is not None:
                k = k_ref[pl.ds(pl.multiple_of(probs[0], tk), tk), :]
            if acc is not None:
                vt = vt_ref[:, pl.ds(pl.multiple_of(acc[0], tk), tk)]
            for c, g in pieces:
                cols = slice(g * QUERY_GROUP, (g + 1) * QUERY_GROUP)
                if probs is not None and (probs[2] is None or g >= probs[2]):
                    s = _dot_nt(k, q_halves[c][cols, :])
                    p = jnp.exp2(s - bound)
                    if probs[2] == g:
                        p = jnp.where(chunk_mask(s.shape, 0), p, 0.0)
                    l_ref[c, :, cols] += jnp.sum(p.reshape(tk // 8, 8, QUERY_GROUP), axis=0)
                    probs[1][c, :, cols] = p.astype(BF16)
                if acc is not None and (acc[2] is None or g >= acc[2]):
                    acc_ref[c, :, cols] += _dot(vt, acc[1][c, :, cols])

        stage(probs=(diag, set_b[0], 0))
        stage(probs=(diag + tk, set_b[1], 1))
        for c in range(2):
            set_b[1][c, :, 0:QUERY_GROUP] = jnp.zeros((tk, QUERY_GROUP), BF16)

        def quad(r, carry):
            base = 4 * r * tk
            first = r == 0
            stage(probs=(base, set_a[0], None),
                  acc=(jnp.where(first, diag, base - 2 * tk), set_b[0], None))
            stage(probs=(base + tk, set_a[1], None),
                  acc=(jnp.where(first, diag + tk, base - tk), set_b[1], None))
            stage(probs=(base + 2 * tk, set_b[0], None), acc=(base, set_a[0], None))
            stage(probs=(base + 3 * tk, set_b[1], None), acc=(base + tk, set_a[1], None))
            return carry

        n_diag = tq // tk
        lax.fori_loop(0, qi * n_diag // 4, quad, 0)
        none_below = qi == 0
        pending = [(jnp.where(none_below, diag, diag - 2 * tk), set_b[0], None),
                   (jnp.where(none_below, diag + tk, diag - tk), set_b[1], None)]
        free, busy = set_a, set_b
        for j in range(2, n_diag, 2):
            filled = [(diag + (j + i) * tk, free[i], j + i) for i in range(2)]
            for i in range(2):
                stage(probs=filled[i], acc=pending[i])
            pending, free, busy = filled, busy, free
        for i in range(2):
            stage(acc=pending[i])

    @pl.when(bound > MAX_STATIC_SHIFT)
    def _():
        tg = GENERAL_KV_TILE

        def tile(start, carry, key_chunk_offset=None):
            start = pl.multiple_of(start, tg)
            k = k_ref[pl.ds(start, tg), :]
            vt = vt_ref[:, pl.ds(start, tg)]
            out = []
            for c in range(2):
                m, l = carry[2 * c], carry[2 * c + 1]
                s = _dot_nt(k, q_halves[c])
                if key_chunk_offset is not None:
                    s = jnp.where(chunk_mask(s.shape, key_chunk_offset), s, NEG_INF)
                m_new = jnp.maximum(m, jnp.max(s, axis=0, keepdims=True))
                alpha = jnp.exp2(m - m_new)
                p = jnp.exp2(s - m_new)
                acc_ref[c] = alpha * acc_ref[c] + _dot(vt, p.astype(BF16))
                out += [m_new, alpha * l + jnp.sum(p, axis=0, keepdims=True)]
            return tuple(out)

        init_m = jnp.full((1, tq), NEG_INF, F32)
        init_l = jnp.zeros((1, tq), F32)
        carry = lax.fori_loop(0, qi * (tq // tg), lambda j, cr: tile(j * tg, cr),
                              (init_m, init_l, init_m, init_l))
        for j in range(tq // tg):
            carry = tile(diag + j * tg, carry, key_chunk_offset=j * (tg // CHUNK))
        l_ref[0, 0:1, :] = carry[1]
        l_ref[1, 0:1, :] = carry[3]

    lam = (jnp.exp(jnp.sum(lam_ref[0:1, :] * lam_ref[1:2, :], axis=-1, keepdims=True))
           - jnp.exp(jnp.sum(lam_ref[2:3, :] * lam_ref[3:4, :], axis=-1, keepdims=True))
           + lam_init)
    l1 = jnp.sum(l_ref[0], axis=0, keepdims=True)
    l2 = jnp.sum(l_ref[1], axis=0, keepdims=True)
    o = acc_ref[0] / l1 - lam * (acc_ref[1] / l2)
    ms = jnp.mean(o * o, axis=0, keepdims=True)
    o = o * lax.rsqrt(ms + EPS)
    o_ref[...] = (o.T * gsub_ref[...] * (1.0 - lam_init)).astype(o_ref.dtype)


def _diff_attn(bound, q, k, vt, lam_vecs, gsub, *, layer, tq, lam_init):
    b, nh, s, dv = q.shape
    assert tq % (4 * KV_TILE) == 0 and tq % GENERAL_KV_TILE == 0 and s % tq == 0
    p_buffer = pltpu.VMEM((2, KV_TILE, tq), BF16)
    return pl.pallas_call(
        functools.partial(_diff_attn_kernel, layer=layer, tq=tq, lam_init=lam_init),
        grid=(b, nh, s // tq),
        in_specs=[pl.BlockSpec(memory_space=pltpu.SMEM),
                  pl.BlockSpec((None, None, tq, dv), lambda bi, hi, qi: (bi, hi, qi, 0)),
                  pl.BlockSpec((None, None, s, dv), lambda bi, hi, qi: (bi, hi, 0, 0)),
                  pl.BlockSpec((None, dv, s), lambda bi, hi, qi: (bi, hi, 0)),
                  _layer_resident(lam_vecs.shape, layer), _layer_resident(gsub.shape, layer)],
        out_specs=pl.BlockSpec((None, tq, dv), lambda bi, hi, qi: (bi, qi, hi)),
        out_shape=jax.ShapeDtypeStruct((b, s, nh * dv), BF16),
        scratch_shapes=[p_buffer, p_buffer, p_buffer, p_buffer,
                        pltpu.VMEM((2, dv, tq), F32), pltpu.VMEM((2, 8, tq), F32)],
        compiler_params=pltpu.CompilerParams(
            dimension_semantics=("arbitrary", "arbitrary", "arbitrary"),
            vmem_limit_bytes=VMEM_LIMIT_BYTES),
        name="diff_attn",
    )(bound, q, k, vt, lam_vecs, gsub)


def _mixer_out_kernel(h_ref, partial_ref, gc_ref, yc_ref, p_ref, w_attn_out_ref, w_o_ref,
                      gmlp_ref, w_up_ref, w_down_ref, gple_ref, w_gate_ref, w_proj_ref,
                      out_ref, *, ff_chunk):
    merged = partial_ref[...] + gc_ref[...] * _dot(yc_ref[...], w_attn_out_ref[...])
    h1 = h_ref[...] + _dot(merged.astype(BF16), w_o_ref[...])

    m = _rms_rows(h1, gmlp_ref[...]).astype(BF16)
    d_ff = w_up_ref.shape[1]
    mlp = jnp.zeros_like(h1)
    for lo in range(0, d_ff, ff_chunk):
        up = _dot(m, w_up_ref[:, lo:lo + ff_chunk])
        act = jnp.square(jnp.maximum(up, 0.0)).astype(BF16)
        mlp = mlp + _dot(act, w_down_ref[lo:lo + ff_chunk, :])
    h2 = h1 + mlp
    embed = _dot(p_ref[...].astype(BF16), w_proj_ref[...])

    e = _rms_rows(h2, gple_ref[...]).astype(BF16)
    gate = jax.nn.sigmoid(_dot(e, w_gate_ref[...]))
    out_ref[...] = h2 + gate * embed


def _mixer_out(h, partial, gc, yc, p, w_attn_out, w_o, gmlp, w_up, w_down, gple, w_gate, w_proj,
               *, layer, tm, ff_chunk):
    b, s, d = h.shape
    lr = lambda a: _layer_resident(a.shape, layer)
    tok = lambda width: pl.BlockSpec((None, tm, width), lambda bi, si: (bi, si, 0))
    p_tok = pl.BlockSpec((None, None, tm, p.shape[-1]), lambda bi, si: (layer, bi, si, 0))
    return pl.pallas_call(
        functools.partial(_mixer_out_kernel, ff_chunk=ff_chunk),
        grid=(b, s // tm),
        in_specs=[tok(d), tok(d), tok(d), tok(yc.shape[-1]), p_tok,
                  lr(w_attn_out), lr(w_o), lr(gmlp), lr(w_up), lr(w_down), lr(gple), lr(w_gate),
                  lr(w_proj)],
        out_specs=tok(d),
        out_shape=jax.ShapeDtypeStruct((b, s, d), F32),
        compiler_params=pltpu.CompilerParams(
            dimension_semantics=("arbitrary", "arbitrary"),
            vmem_limit_bytes=VMEM_LIMIT_BYTES),
        name="mixer_out",
    )(h, partial, gc, yc, p, w_attn_out, w_o, gmlp, w_up, w_down, gple, w_gate, w_proj)


def _block_diag(blocks):
    g, n, _ = blocks.shape
    eye = jnp.eye(g, dtype=blocks.dtype)
    return (eye[:, None, :, None] * blocks[:, :, None, :]).reshape(g * n, g * n)


def kernel(x, p, norm_mix_g, w_in, pool_w, pool_scale, conv_w, q_norm_g, k_norm_g, lam_q1, lam_k1,
           lam_q2, lam_k2, sub_norm_g, w_pool_out, w_conv_out, w_attn_out, w_o, norm_mlp_g, w_up,
           w_down, norm_ple_g, w_ple_gate, w_ple_proj):
    depth = w_in.shape[0]
    c = MIX_WIDTH
    tm_in, tm_out, tq, ff_chunk = 512, 512, 2048, 1024
    n_seg = c // ATTN_HEAD_DIM
    seg = _block_diag(jnp.ones((n_seg, ATTN_HEAD_DIM, ATTN_HEAD_DIM), BF16))
    rows = lambda v: v.reshape(depth, 1, -1).astype(F32)
    bf = lambda w: w.astype(BF16)
    score_scale = ATTN_HEAD_DIM ** -0.5 * math.log2(math.e)

    gq = rows(jnp.tile(q_norm_g, (1, n_seg)) * score_scale)
    gk = rows(jnp.tile(k_norm_g, (1, n_seg)))
    pool_bd = bf(jax.vmap(_block_diag)(pool_w))
    lam_vecs = jnp.stack([lam_q1, lam_k1, lam_q2, lam_k2], axis=1).astype(F32)
    score_bound = (64.0 * SCORE_BOUND_MARGIN * score_scale * jnp.max(jnp.abs(q_norm_g), axis=-1)
                   * jnp.max(jnp.abs(k_norm_g), axis=-1)).astype(F32)
    mixer_in_params = (rows(norm_mix_g), bf(w_in), seg, gq, gk, pool_bd, rows(pool_scale),
                       conv_w.astype(F32), bf(w_pool_out), bf(w_conv_out))
    mixer_out_params = (bf(w_attn_out), bf(w_o), rows(norm_mlp_g), bf(w_up), bf(w_down),
                        rows(norm_ple_g), bf(w_ple_gate), bf(w_ple_proj))
    gsub = rows(sub_norm_g)

    h = x
    for i in range(depth):
        lam_init = 0.8 - 0.6 * math.exp(-0.3 * i)
        partial, gc, q, k, vt = _mixer_in(h, *mixer_in_params, layer=i, tm=tm_in)
        yc = _diff_attn(score_bound, q, k, vt, lam_vecs, gsub, layer=i, tq=tq, lam_init=lam_init)
        h = _mixer_out(h, partial, gc, yc, p, *mixer_out_params, layer=i, tm=tm_out,
                       ff_chunk=ff_chunk)
    return h
```

```python
import functools
import math

import jax
import jax.numpy as jnp
from jax import lax
from jax.experimental import pallas as pl
from jax.experimental.pallas import tpu as pltpu

MIX_WIDTH = 512
POOL_WINDOWS = (2, 4, 8, 16)
POOL_GROUP_DIM = MIX_WIDTH // len(POOL_WINDOWS)
CONV_WIDTH = 3
ATTN_HEADS = 4
ATTN_HEAD_DIM = 64
ATTN_V_DIM = 2 * ATTN_HEAD_DIM
CHUNK = 64
MXU_TILE = 256
QUERY_GROUP = MXU_TILE
KV_TILE = 256
GENERAL_KV_TILE = 256
MAX_STATIC_SHIFT = 30.0
SCORE_BOUND_MARGIN = 1.02
EPS = 1e-6
NEG_INF = -1e30

POOL_HALO = 16
CONV_HALO = 8
VMEM_LIMIT_BYTES = 56 * 1024 * 1024

F32 = jnp.float32
BF16 = jnp.bfloat16


def _dot(a, b):
    return jnp.dot(a, b, preferred_element_type=F32)


def _dot_nt(a, b):
    return lax.dot_general(a, b, (((1,), (1,)), ((), ())), preferred_element_type=F32)


def _block_diag_dot(x, w_ref):
    n = w_ref.shape[0]
    tiles = [slice(lo, lo + MXU_TILE) for lo in range(0, n, MXU_TILE)]
    return jnp.concatenate([_dot(x[:, t], w_ref[t, t]) for t in tiles], axis=-1)


def _rms_rows(x, g):
    ms = jnp.mean(x * x, axis=-1, keepdims=True)
    return x * lax.rsqrt(ms + EPS) * g


def _resident(shape):
    nd = len(shape)
    return pl.BlockSpec(shape, lambda *_: (0,) * nd, pipeline_mode=pl.Buffered(1))


def _layer_resident(stacked_shape, layer):
    nd = len(stacked_shape)
    return pl.BlockSpec((None,) + tuple(stacked_shape[1:]), lambda *_: (layer,) + (0,) * (nd - 1),
                        pipeline_mode=pl.Buffered(1))


def _mixer_in_kernel(h_ref, gmix_ref, w_in_ref, seg_ref, gq_ref, gk_ref,
                     pool_w_ref, pool_scale_ref, conv_w_ref, w_pool_out_ref, w_conv_out_ref,
                     partial_ref, gc_ref, q_ref, k_ref, vt_ref,
                     abuf_ref, zbuf_ref, vbuf_ref, *, tm):
    s_idx = pl.program_id(1)
    c = MIX_WIDTH

    @pl.when(s_idx == 0)
    def _():
        abuf_ref[0:POOL_HALO, :] = jnp.zeros((POOL_HALO, c), F32)
        zbuf_ref[0:CONV_HALO, :] = jnp.zeros((CONV_HALO, c), F32)

    u = _rms_rows(h_ref[...], gmix_ref[...]).astype(BF16)

    def proj(lo, width):
        return _dot(u, w_in_ref[:, lo:lo + width])

    gate_lo = 7 * c
    d = partial_ref.shape[-1]
    a = proj(0, c)
    abuf_ref[POOL_HALO:POOL_HALO + tm, :] = a
    z = proj(3 * c, c) * proj(c, c)
    zbuf_ref[CONV_HALO:CONV_HALO + tm, :] = z
    gc_ref[...] = jax.nn.sigmoid(proj(gate_lo + 2 * d, d))

    t_pos = (s_idx * tm + lax.broadcasted_iota(jnp.int32, (tm, POOL_GROUP_DIM), 0) + 1).astype(F32)
    pooled = []
    for g, win in enumerate(POOL_WINDOWS):
        lo = g * POOL_GROUP_DIM
        wsum = a[:, lo:lo + POOL_GROUP_DIM]
        for back in range(1, win):
            wsum = wsum + abuf_ref[POOL_HALO - back:POOL_HALO - back + tm, lo:lo + POOL_GROUP_DIM]
        cnt = jnp.minimum(t_pos, float(win))
        pooled.append(wsum / cnt - a[:, lo:lo + POOL_GROUP_DIM])
    pooled = jnp.concatenate(pooled, axis=-1).astype(BF16)
    y_a = _block_diag_dot(pooled, pool_w_ref) * pool_scale_ref[...]
    abuf_ref[0:POOL_HALO, :] = abuf_ref[tm:tm + POOL_HALO, :]

    vbuf_ref[...] = proj(6 * c, c)
    vt_ref[...] = vbuf_ref[...].T.astype(BF16)

    conv = conv_w_ref[CONV_WIDTH - 1:CONV_WIDTH, :] * z
    for back in range(1, CONV_WIDTH):
        tap = conv_w_ref[CONV_WIDTH - 1 - back:CONV_WIDTH - back, :]
        conv = conv + tap * zbuf_ref[CONV_HALO - back:CONV_HALO - back + tm, :]
    y_b = proj(2 * c, c) * conv
    zbuf_ref[0:CONV_HALO, :] = zbuf_ref[tm:tm + CONV_HALO, :]

    g_a = jax.nn.sigmoid(proj(gate_lo, d))
    part = g_a * _dot(y_a.astype(BF16), w_pool_out_ref[...])
    g_b = jax.nn.sigmoid(proj(gate_lo + d, d))
    part = part + g_b * _dot(y_b.astype(BF16), w_conv_out_ref[...])
    partial_ref[...] = part

    def qk_norm(x, g):
        ss = _block_diag_dot((x * x).astype(BF16), seg_ref)
        return (x * lax.rsqrt(ss * (1.0 / ATTN_HEAD_DIM) + EPS) * g).astype(BF16)

    qn = qk_norm(proj(4 * c, c), gq_ref[...])
    kn = qk_norm(proj(5 * c, c), gk_ref[...])
    for hd in range(ATTN_HEADS):
        q_ref[hd] = qn[:, hd * ATTN_V_DIM:(hd + 1) * ATTN_V_DIM]
        k_ref[hd] = kn[:, hd * ATTN_V_DIM:(hd + 1) * ATTN_V_DIM]


def _mixer_in(h, gmix, w_in, seg, gq, gk, pool_w, pool_scale, conv_w, w_pool_out,
              w_conv_out, *, layer, tm):
    b, s, d = h.shape
    lr = lambda a: _layer_resident(a.shape, layer)
    c = MIX_WIDTH
    tok = lambda width: pl.BlockSpec((None, tm, width), lambda bi, si: (bi, si, 0))
    head_tok = pl.BlockSpec((None, ATTN_HEADS, tm, ATTN_V_DIM), lambda bi, si: (bi, 0, si, 0))
    return pl.pallas_call(
        functools.partial(_mixer_in_kernel, tm=tm),
        grid=(b, s // tm),
        in_specs=[tok(d), lr(gmix), lr(w_in), _resident(seg.shape), lr(gq), lr(gk),
                  lr(pool_w), lr(pool_scale), lr(conv_w), lr(w_pool_out), lr(w_conv_out)],
        out_specs=[tok(d), tok(d), head_tok, head_tok,
                   pl.BlockSpec((None, c, tm), lambda bi, si: (bi, 0, si))],
        out_shape=[jax.ShapeDtypeStruct((b, s, d), F32),
                   jax.ShapeDtypeStruct((b, s, d), F32),
                   jax.ShapeDtypeStruct((b, ATTN_HEADS, s, ATTN_V_DIM), BF16),
                   jax.ShapeDtypeStruct((b, ATTN_HEADS, s, ATTN_V_DIM), BF16),
                   jax.ShapeDtypeStruct((b, c, s), BF16)],
        scratch_shapes=[pltpu.VMEM((POOL_HALO + tm, c), F32),
                        pltpu.VMEM((CONV_HALO + tm, c), F32),
                        pltpu.VMEM((tm, c), F32)],
        compiler_params=pltpu.CompilerParams(
            dimension_semantics=("arbitrary", "arbitrary"),
            vmem_limit_bytes=VMEM_LIMIT_BYTES),
        name="mixer_in",
    )(h, gmix, w_in, seg, gq, gk, pool_w, pool_scale, conv_w, w_pool_out, w_conv_out)


def _diff_attn_kernel(bound_ref, q_ref, k_ref, vt_ref, lam_ref, gsub_ref, o_ref,
                      pa0_ref, pa1_ref, pb0_ref, pb1_ref, acc_ref, l_ref, *, layer, tq, lam_init):
    qi = pl.program_id(2)
    q = q_ref[...]
    lane = lax.broadcasted_iota(jnp.int32, q.shape, 1)
    zero = jnp.zeros_like(q)
    q_halves = (jnp.where(lane < ATTN_HEAD_DIM, q, zero), jnp.where(lane >= ATTN_HEAD_DIM, q, zero))
    bound = bound_ref[layer]
    diag = qi * tq
    n_groups = tq // QUERY_GROUP
    acc_ref[...] = jnp.zeros_like(acc_ref)
    l_ref[...] = jnp.zeros_like(l_ref)

    def chunk_mask(shape, key_chunk_offset):
        key_chunk = lax.broadcasted_iota(jnp.int32, shape, 0) // CHUNK + key_chunk_offset
        q_chunk = lax.broadcasted_iota(jnp.int32, shape, 1) // CHUNK
        return key_chunk <= q_chunk

    lam = (jnp.exp(jnp.sum(lam_ref[0:1, :] * lam_ref[1:2, :], axis=-1, keepdims=True))
           - jnp.exp(jnp.sum(lam_ref[2:3, :] * lam_ref[3:4, :], axis=-1, keepdims=True))
           + lam_init)

    def finalize(g):
        cols = slice(g * QUERY_GROUP, (g + 1) * QUERY_GROUP)
        l1 = jnp.sum(l_ref[0, :, cols], axis=0, keepdims=True)
        l2 = jnp.sum(l_ref[1, :, cols], axis=0, keepdims=True)
        o = acc_ref[0, :, cols] / l1 - lam * (acc_ref[1, :, cols] / l2)
        ms = jnp.mean(o * o, axis=0, keepdims=True)
        o = o * lax.rsqrt(ms + EPS)
        o_ref[cols, :] = (o.T * gsub_ref[...] * (1.0 - lam_init)).astype(o_ref.dtype)

    @pl.when(bound <= MAX_STATIC_SHIFT)
    def _():
        tk = KV_TILE
        set_a, set_b = (pa0_ref, pa1_ref), (pb0_ref, pb1_ref)
        pieces = [(c, g) for c in range(2) for g in range(n_groups)]

        def stage(probs=None, acc=None):
            if probs is not None:
                k = k_ref[pl.ds(pl.multiple_of(probs[0], tk), tk), :]
            if acc is not None:
                vt = vt_ref[:, pl.ds(pl.multiple_of(acc[0], tk), tk)]
            for c, g in pieces:
                cols = slice(g * QUERY_GROUP, (g + 1) * QUERY_GROUP)
                if probs is not None and (probs[2] is None or g >= probs[2]):
                    s = _dot_nt(k, q_halves[c][cols, :])
                    p = jnp.exp2(s - bound)
                    if probs[2] == g:
                        p = jnp.where(chunk_mask(s.shape, 0), p, 0.0)
                    l_ref[c, :, cols] += jnp.sum(p.reshape(tk // 8, 8, QUERY_GROUP), axis=0)
                    probs[1][c, :, cols] = p.astype(BF16)
                if acc is not None and (acc[2] is None or g >= acc[2]):
                    acc_ref[c, :, cols] += _dot(vt, acc[1][c, :, cols])

        stage(probs=(diag, set_b[0], 0))
        stage(probs=(diag + tk, set_b[1], 1))
        for c in range(2):
            set_b[1][c, :, 0:QUERY_GROUP] = jnp.zeros((tk, QUERY_GROUP), BF16)

        def quad(r, carry):
            base = 4 * r * tk
            first = r == 0
            stage(probs=(base, set_a[0], None),
                  acc=(jnp.where(first, diag, base - 2 * tk), set_b[0], None))
            stage(probs=(base + tk, set_a[1], None),
                  acc=(jnp.where(first, diag + tk, base - tk), set_b[1], None))
            stage(probs=(base + 2 * tk, set_b[0], None), acc=(base, set_a[0], None))
            stage(probs=(base + 3 * tk, set_b[1], None), acc=(base + tk, set_a[1], None))
            return carry

        n_diag = tq // tk
        lax.fori_loop(0, qi * n_diag // 4, quad, 0)
        none_below = qi == 0
        pending = [(jnp.where(none_below, diag, diag - 2 * tk), set_b[0], None),
                   (jnp.where(none_below, diag + tk, diag - tk), set_b[1], None)]
        free, busy = set_a, set_b
        for j in range(2, n_diag, 2):
            filled = [(diag + (j + i) * tk, free[i], j + i) for i in range(2)]
            for i in range(2):
                stage(probs=filled[i], acc=pending[i])
            pending, free, busy = filled, busy, free
            finalize(j - 2)
            finalize(j - 1)
        for i in range(2):
            stage(acc=pending[i])
        finalize(n_diag - 2)
        finalize(n_diag - 1)

    @pl.when(bound > MAX_STATIC_SHIFT)
    def _():
        tg = GENERAL_KV_TILE

        def tile(start, carry, key_chunk_offset=None):
            start = pl.multiple_of(start, tg)
            k = k_ref[pl.ds(start, tg), :]
            vt = vt_ref[:, pl.ds(start, tg)]
            out = []
            for c in range(2):
                m, l = carry[2 * c], carry[2 * c + 1]
                s = _dot_nt(k, q_halves[c])
                if key_chunk_offset is not None:
                    s = jnp.where(chunk_mask(s.shape, key_chunk_offset), s, NEG_INF)
                m_new = jnp.maximum(m, jnp.max(s, axis=0, keepdims=True))
                alpha = jnp.exp2(m - m_new)
                p = jnp.exp2(s - m_new)
                acc_ref[c] = alpha * acc_ref[c] + _dot(vt, p.astype(BF16))
                out += [m_new, alpha * l + jnp.sum(p, axis=0, keepdims=True)]
            return tuple(out)

        init_m = jnp.full((1, tq), NEG_INF, F32)
        init_l = jnp.zeros((1, tq), F32)
        carry = lax.fori_loop(0, qi * (tq // tg), lambda j, cr: tile(j * tg, cr),
                              (init_m, init_l, init_m, init_l))
        for j in range(tq // tg):
            carry = tile(diag + j * tg, carry, key_chunk_offset=j * (tg // CHUNK))
        l_ref[0, 0:1, :] = carry[1]
        l_ref[1, 0:1, :] = carry[3]
        for g in range(n_groups):
            finalize(g)


def _diff_attn(bound, q, k, vt, lam_vecs, gsub, *, layer, tq, lam_init):
    b, nh, s, dv = q.shape
    assert tq % (4 * KV_TILE) == 0 and tq % GENERAL_KV_TILE == 0 and s % tq == 0
    assert QUERY_GROUP == KV_TILE
    p_buffer = pltpu.VMEM((2, KV_TILE, tq), BF16)
    return pl.pallas_call(
        functools.partial(_diff_attn_kernel, layer=layer, tq=tq, lam_init=lam_init),
        grid=(b, nh, s // tq),
        in_specs=[pl.BlockSpec(memory_space=pltpu.SMEM),
                  pl.BlockSpec((None, None, tq, dv), lambda bi, hi, qi: (bi, hi, qi, 0)),
                  pl.BlockSpec((None, None, s, dv), lambda bi, hi, qi: (bi, hi, 0, 0)),
                  pl.BlockSpec((None, dv, s), lambda bi, hi, qi: (bi, hi, 0)),
                  _layer_resident(lam_vecs.shape, layer), _layer_resident(gsub.shape, layer)],
        out_specs=pl.BlockSpec((None, tq, dv), lambda bi, hi, qi: (bi, qi, hi)),
        out_shape=jax.ShapeDtypeStruct((b, s, nh * dv), BF16),
        scratch_shapes=[p_buffer, p_buffer, p_buffer, p_buffer,
                        pltpu.VMEM((2, dv, tq), F32), pltpu.VMEM((2, 8, tq), F32)],
        compiler_params=pltpu.CompilerParams(
            dimension_semantics=("arbitrary", "arbitrary", "arbitrary"),
            vmem_limit_bytes=VMEM_LIMIT_BYTES),
        name="diff_attn",
    )(bound, q, k, vt, lam_vecs, gsub)


def _mixer_out_kernel(h_ref, partial_ref, gc_ref, yc_ref, p_ref, w_attn_out_ref, w_o_ref,
                      gmlp_ref, w_up_ref, w_down_ref, gple_ref, w_gate_ref, w_proj_ref,
                      out_ref, *, ff_chunk):
    merged = partial_ref[...] + gc_ref[...] * _dot(yc_ref[...], w_attn_out_ref[...])
    h1 = h_ref[...] + _dot(merged.astype(BF16), w_o_ref[...])

    m = _rms_rows(h1, gmlp_ref[...]).astype(BF16)
    d_ff = w_up_ref.shape[1]
    mlp = jnp.zeros_like(h1)
    for lo in range(0, d_ff, ff_chunk):
        up = _dot(m, w_up_ref[:, lo:lo + ff_chunk])
        act = jnp.square(jnp.maximum(up, 0.0)).astype(BF16)
        mlp = mlp + _dot(act, w_down_ref[lo:lo + ff_chunk, :])
    h2 = h1 + mlp
    embed = _dot(p_ref[...].astype(BF16), w_proj_ref[...])

    e = _rms_rows(h2, gple_ref[...]).astype(BF16)
    gate = jax.nn.sigmoid(_dot(e, w_gate_ref[...]))
    out_ref[...] = h2 + gate * embed


def _mixer_out(h, partial, gc, yc, p, w_attn_out, w_o, gmlp, w_up, w_down, gple, w_gate, w_proj,
               *, layer, tm, ff_chunk):
    b, s, d = h.shape
    lr = lambda a: _layer_resident(a.shape, layer)
    tok = lambda width: pl.BlockSpec((None, tm, width), lambda bi, si: (bi, si, 0))
    p_tok = pl.BlockSpec((None, None, tm, p.shape[-1]), lambda bi, si: (layer, bi, si, 0))
    return pl.pallas_call(
        functools.partial(_mixer_out_kernel, ff_chunk=ff_chunk),
        grid=(b, s // tm),
        in_specs=[tok(d), tok(d), tok(d), tok(yc.shape[-1]), p_tok,
                  lr(w_attn_out), lr(w_o), lr(gmlp), lr(w_up), lr(w_down), lr(gple), lr(w_gate),
                  lr(w_proj)],
        out_specs=tok(d),
        out_shape=jax.ShapeDtypeStruct((b, s, d), F32),
        compiler_params=pltpu.CompilerParams(
            dimension_semantics=("arbitrary", "arbitrary"),
            vmem_limit_bytes=VMEM_LIMIT_BYTES),
        name="mixer_out",
    )(h, partial, gc, yc, p, w_attn_out, w_o, gmlp, w_up, w_down, gple, w_gate, w_proj)


def _block_diag(blocks):
    g, n, _ = blocks.shape
    eye = jnp.eye(g, dtype=blocks.dtype)
    return (eye[:, None, :, None] * blocks[:, :, None, :]).reshape(g * n, g * n)


def kernel(x, p, norm_mix_g, w_in, pool_w, pool_scale, conv_w, q_norm_g, k_norm_g, lam_q1, lam_k1,
           lam_q2, lam_k2, sub_norm_g, w_pool_out, w_conv_out, w_attn_out, w_o, norm_mlp_g, w_up,
           w_down, norm_ple_g, w_ple_gate, w_ple_proj):
    depth = w_in.shape[0]
    c = MIX_WIDTH
    tm_in, tm_out, tq, ff_chunk = 512, 512, 2048, 1024
    n_seg = c // ATTN_HEAD_DIM
    seg = _block_diag(jnp.ones((n_seg, ATTN_HEAD_DIM, ATTN_HEAD_DIM), BF16))
    rows = lambda v: v.reshape(depth, 1, -1).astype(F32)
    bf = lambda w: w.astype(BF16)
    score_scale = ATTN_HEAD_DIM ** -0.5 * math.log2(math.e)

    gq = rows(jnp.tile(q_norm_g, (1, n_seg)) * score_scale)
    gk = rows(jnp.tile(k_norm_g, (1, n_seg)))
    pool_bd = bf(jax.vmap(_block_diag)(pool_w))
    lam_vecs = jnp.stack([lam_q1, lam_k1, lam_q2, lam_k2], axis=1).astype(F32)
    score_bound = (64.0 * SCORE_BOUND_MARGIN * score_scale * jnp.max(jnp.abs(q_norm_g), axis=-1)
                   * jnp.max(jnp.abs(k_norm_g), axis=-1)).astype(F32)
    mixer_in_params = (rows(norm_mix_g), bf(w_in), seg, gq, gk, pool_bd, rows(pool_scale),
                       conv_w.astype(F32), bf(w_pool_out), bf(w_conv_out))
    mixer_out_params = (bf(w_attn_out), bf(w_o), rows(norm_mlp_g), bf(w_up), bf(w_down),
                        rows(norm_ple_g), bf(w_ple_gate), bf(w_ple_proj))
    gsub = rows(sub_norm_g)

    h = x
    for i in range(depth):
        lam_init = 0.8 - 0.6 * math.exp(-0.3 * i)
        partial, gc, q, k, vt = _mixer_in(h, *mixer_in_params, layer=i, tm=tm_in)
        yc = _diff_attn(score_bound, q, k, vt, lam_vecs, gsub, layer=i, tq=tq, lam_init=lam_init)
        h = _mixer_out(h, partial, gc, yc, p, *mixer_out_params, layer=i, tm=tm_out,
                       ff_chunk=ff_chunk)
    return h
```

```python
import functools
import math

import jax
import jax.numpy as jnp
from jax import lax
from jax.experimental import pallas as pl
from jax.experimental.pallas import tpu as pltpu

MIX_WIDTH = 512
POOL_WINDOWS = (2, 4, 8, 16)
POOL_GROUP_DIM = MIX_WIDTH // len(POOL_WINDOWS)
CONV_WIDTH = 3
ATTN_HEADS = 4
ATTN_HEAD_DIM = 64
ATTN_V_DIM = 2 * ATTN_HEAD_DIM
CHUNK = 64
MXU_TILE = 256
QUERY_GROUP = MXU_TILE
KV_TILE = 256
GENERAL_KV_TILE = 256
MAX_STATIC_SHIFT = -1.0
SCORE_BOUND_MARGIN = 1.02
EPS = 1e-6
NEG_INF = -1e30

POOL_HALO = 16
CONV_HALO = 8
VMEM_LIMIT_BYTES = 56 * 1024 * 1024

F32 = jnp.float32
BF16 = jnp.bfloat16


def _dot(a, b):
    return jnp.dot(a, b, preferred_element_type=F32)


def _dot_nt(a, b):
    return lax.dot_general(a, b, (((1,), (1,)), ((), ())), preferred_element_type=F32)


def _block_diag_dot(x, w_ref):
    n = w_ref.shape[0]
    tiles = [slice(lo, lo + MXU_TILE) for lo in range(0, n, MXU_TILE)]
    return jnp.concatenate([_dot(x[:, t], w_ref[t, t]) for t in tiles], axis=-1)


def _rms_rows(x, g):
    ms = jnp.mean(x * x, axis=-1, keepdims=True)
    return x * lax.rsqrt(ms + EPS) * g


def _resident(shape):
    nd = len(shape)
    return pl.BlockSpec(shape, lambda *_: (0,) * nd, pipeline_mode=pl.Buffered(1))


def _layer_resident(stacked_shape, layer):
    nd = len(stacked_shape)
    return pl.BlockSpec((None,) + tuple(stacked_shape[1:]), lambda *_: (layer,) + (0,) * (nd - 1),
                        pipeline_mode=pl.Buffered(1))


def _mixer_in_kernel(h_ref, gmix_ref, w_in_ref, seg_ref, gq_ref, gk_ref,
                     pool_w_ref, pool_scale_ref, conv_w_ref, w_pool_out_ref, w_conv_out_ref,
                     partial_ref, gc_ref, q_ref, k_ref, vt_ref,
                     abuf_ref, zbuf_ref, vbuf_ref, *, tm):
    s_idx = pl.program_id(1)
    c = MIX_WIDTH

    @pl.when(s_idx == 0)
    def _():
        abuf_ref[0:POOL_HALO, :] = jnp.zeros((POOL_HALO, c), F32)
        zbuf_ref[0:CONV_HALO, :] = jnp.zeros((CONV_HALO, c), F32)

    u = _rms_rows(h_ref[...], gmix_ref[...]).astype(BF16)

    def proj(lo, width):
        return _dot(u, w_in_ref[:, lo:lo + width])

    gate_lo = 7 * c
    d = partial_ref.shape[-1]
    a = proj(0, c)
    abuf_ref[POOL_HALO:POOL_HALO + tm, :] = a
    z = proj(3 * c, c) * proj(c, c)
    zbuf_ref[CONV_HALO:CONV_HALO + tm, :] = z
    gc_ref[...] = jax.nn.sigmoid(proj(gate_lo + 2 * d, d))

    t_pos = (s_idx * tm + lax.broadcasted_iota(jnp.int32, (tm, POOL_GROUP_DIM), 0) + 1).astype(F32)
    pooled = []
    for g, win in enumerate(POOL_WINDOWS):
        lo = g * POOL_GROUP_DIM
        wsum = a[:, lo:lo + POOL_GROUP_DIM]
        for back in range(1, win):
            wsum = wsum + abuf_ref[POOL_HALO - back:POOL_HALO - back + tm, lo:lo + POOL_GROUP_DIM]
        cnt = jnp.minimum(t_pos, float(win))
        pooled.append(wsum / cnt - a[:, lo:lo + POOL_GROUP_DIM])
    pooled = jnp.concatenate(pooled, axis=-1).astype(BF16)
    y_a = _block_diag_dot(pooled, pool_w_ref) * pool_scale_ref[...]
    abuf_ref[0:POOL_HALO, :] = abuf_ref[tm:tm + POOL_HALO, :]

    vbuf_ref[...] = proj(6 * c, c)
    vt_ref[...] = vbuf_ref[...].T.astype(BF16)

    conv = conv_w_ref[CONV_WIDTH - 1:CONV_WIDTH, :] * z
    for back in range(1, CONV_WIDTH):
        tap = conv_w_ref[CONV_WIDTH - 1 - back:CONV_WIDTH - back, :]
        conv = conv + tap * zbuf_ref[CONV_HALO - back:CONV_HALO - back + tm, :]
    y_b = proj(2 * c, c) * conv
    zbuf_ref[0:CONV_HALO, :] = zbuf_ref[tm:tm + CONV_HALO, :]

    g_a = jax.nn.sigmoid(proj(gate_lo, d))
    part = g_a * _dot(y_a.astype(BF16), w_pool_out_ref[...])
    g_b = jax.nn.sigmoid(proj(gate_lo + d, d))
    part = part + g_b * _dot(y_b.astype(BF16), w_conv_out_ref[...])
    partial_ref[...] = part

    def qk_norm(x, g):
        ss = _block_diag_dot((x * x).astype(BF16), seg_ref)
        return (x * lax.rsqrt(ss * (1.0 / ATTN_HEAD_DIM) + EPS) * g).astype(BF16)

    qn = qk_norm(proj(4 * c, c), gq_ref[...])
    kn = qk_norm(proj(5 * c, c), gk_ref[...])
    for hd in range(ATTN_HEADS):
        q_ref[hd] = qn[:, hd * ATTN_V_DIM:(hd + 1) * ATTN_V_DIM]
        k_ref[hd] = kn[:, hd * ATTN_V_DIM:(hd + 1) * ATTN_V_DIM]


def _mixer_in(h, gmix, w_in, seg, gq, gk, pool_w, pool_scale, conv_w, w_pool_out,
              w_conv_out, *, layer, tm):
    b, s, d = h.shape
    lr = lambda a: _layer_resident(a.shape, layer)
    c = MIX_WIDTH
    tok = lambda width: pl.BlockSpec((None, tm, width), lambda bi, si: (bi, si, 0))
    head_tok = pl.BlockSpec((None, ATTN_HEADS, tm, ATTN_V_DIM), lambda bi, si: (bi, 0, si, 0))
    return pl.pallas_call(
        functools.partial(_mixer_in_kernel, tm=tm),
        grid=(b, s // tm),
        in_specs=[tok(d), lr(gmix), lr(w_in), _resident(seg.shape), lr(gq), lr(gk),
                  lr(pool_w), lr(pool_scale), lr(conv_w), lr(w_pool_out), lr(w_conv_out)],
        out_specs=[tok(d), tok(d), head_tok, head_tok,
                   pl.BlockSpec((None, c, tm), lambda bi, si: (bi, 0, si))],
        out_shape=[jax.ShapeDtypeStruct((b, s, d), F32),
                   jax.ShapeDtypeStruct((b, s, d), F32),
                   jax.ShapeDtypeStruct((b, ATTN_HEADS, s, ATTN_V_DIM), BF16),
                   jax.ShapeDtypeStruct((b, ATTN_HEADS, s, ATTN_V_DIM), BF16),
                   jax.ShapeDtypeStruct((b, c, s), BF16)],
        scratch_shapes=[pltpu.VMEM((POOL_HALO + tm, c), F32),
                        pltpu.VMEM((CONV_HALO + tm, c), F32),
                        pltpu.VMEM((tm, c), F32)],
        compiler_params=pltpu.CompilerParams(
            dimension_semantics=("arbitrary", "arbitrary"),
            vmem_limit_bytes=VMEM_LIMIT_BYTES),
        name="mixer_in",
    )(h, gmix, w_in, seg, gq, gk, pool_w, pool_scale, conv_w, w_pool_out, w_conv_out)


def _diff_attn_kernel(bound_ref, q_ref, k_ref, vt_ref, lam_ref, gsub_ref, o_ref,
                      pa0_ref, pa1_ref, pb0_ref, pb1_ref, acc_ref, l_ref, *, layer, tq, lam_init):
    qi = pl.program_id(2)
    q = q_ref[...]
    lane = lax.broadcasted_iota(jnp.int32, q.shape, 1)
    zero = jnp.zeros_like(q)
    q_halves = (jnp.where(lane < ATTN_HEAD_DIM, q, zero), jnp.where(lane >= ATTN_HEAD_DIM, q, zero))
    bound = bound_ref[layer]
    diag = qi * tq
    n_groups = tq // QUERY_GROUP
    acc_ref[...] = jnp.zeros_like(acc_ref)
    l_ref[...] = jnp.zeros_like(l_ref)

    def chunk_mask(shape, key_chunk_offset):
        key_chunk = lax.broadcasted_iota(jnp.int32, shape, 0) // CHUNK + key_chunk_offset
        q_chunk = lax.broadcasted_iota(jnp.int32, shape, 1) // CHUNK
        return key_chunk <= q_chunk

    lam = (jnp.exp(jnp.sum(lam_ref[0:1, :] * lam_ref[1:2, :], axis=-1, keepdims=True))
           - jnp.exp(jnp.sum(lam_ref[2:3, :] * lam_ref[3:4, :], axis=-1, keepdims=True))
           + lam_init)

    def finalize(g):
        cols = slice(g * QUERY_GROUP, (g + 1) * QUERY_GROUP)
        l1 = jnp.sum(l_ref[0, :, cols], axis=0, keepdims=True)
        l2 = jnp.sum(l_ref[1, :, cols], axis=0, keepdims=True)
        o = acc_ref[0, :, cols] / l1 - lam * (acc_ref[1, :, cols] / l2)
        ms = jnp.mean(o * o, axis=0, keepdims=True)
        o = o * lax.rsqrt(ms + EPS)
        o_ref[cols, :] = (o.T * gsub_ref[...] * (1.0 - lam_init)).astype(o_ref.dtype)

    @pl.when(bound <= MAX_STATIC_SHIFT)
    def _():
        tk = KV_TILE
        set_a, set_b = (pa0_ref, pa1_ref), (pb0_ref, pb1_ref)
        pieces = [(c, g) for c in range(2) for g in range(n_groups)]

        def stage(probs=None, acc=None):
            if probs is not None:
                k = k_ref[pl.ds(pl.multiple_of(probs[0], tk), tk), :]
            if acc is not None:
                vt = vt_ref[:, pl.ds(pl.multiple_of(acc[0], tk), tk)]
            for c, g in pieces:
                cols = slice(g * QUERY_GROUP, (g + 1) * QUERY_GROUP)
                if probs is not None and (probs[2] is None or g >= probs[2]):
                    s = _dot_nt(k, q_halves[c][cols, :])
                    p = jnp.exp2(s - bound)
                    if probs[2] == g:
                        p = jnp.where(chunk_mask(s.shape, 0), p, 0.0)
                    l_ref[c, :, cols] += jnp.sum(p.reshape(tk // 8, 8, QUERY_GROUP), axis=0)
                    probs[1][c, :, cols] = p.astype(BF16)
                if acc is not None and (acc[2] is None or g >= acc[2]):
                    acc_ref[c, :, cols] += _dot(vt, acc[1][c, :, cols])

        stage(probs=(diag, set_b[0], 0))
        stage(probs=(diag + tk, set_b[1], 1))
        for c in range(2):
            set_b[1][c, :, 0:QUERY_GROUP] = jnp.zeros((tk, QUERY_GROUP), BF16)

        def quad(r, carry):
            base = 4 * r * tk
            first = r == 0
            stage(probs=(base, set_a[0], None),
                  acc=(jnp.where(first, diag, base - 2 * tk), set_b[0], None))
            stage(probs=(base + tk, set_a[1], None),
                  acc=(jnp.where(first, diag + tk, base - tk), set_b[1], None))
            stage(probs=(base + 2 * tk, set_b[0], None), acc=(base, set_a[0], None))
            stage(probs=(base + 3 * tk, set_b[1], None), acc=(base + tk, set_a[1], None))
            return carry

        n_diag = tq // tk
        lax.fori_loop(0, qi * n_diag // 4, quad, 0)
        none_below = qi == 0
        pending = [(jnp.where(none_below, diag, diag - 2 * tk), set_b[0], None),
                   (jnp.where(none_below, diag + tk, diag - tk), set_b[1], None)]
        free, busy = set_a, set_b
        for j in range(2, n_diag, 2):
            filled = [(diag + (j + i) * tk, free[i], j + i) for i in range(2)]
            for i in range(2):
                stage(probs=filled[i], acc=pending[i])
            pending, free, busy = filled, busy, free
            finalize(j - 2)
            finalize(j - 1)
        for i in range(2):
            stage(acc=pending[i])
        finalize(n_diag - 2)
        finalize(n_diag - 1)

    @pl.when(bound > MAX_STATIC_SHIFT)
    def _():
        tg = GENERAL_KV_TILE

        def tile(start, carry, key_chunk_offset=None):
            start = pl.multiple_of(start, tg)
            k = k_ref[pl.ds(start, tg), :]
            vt = vt_ref[:, pl.ds(start, tg)]
            out = []
            for c in range(2):
                m, l = carry[2 * c], carry[2 * c + 1]
                s = _dot_nt(k, q_halves[c])
                if key_chunk_offset is not None:
                    s = jnp.where(chunk_mask(s.shape, key_chunk_offset), s, NEG_INF)
                m_new = jnp.maximum(m, jnp.max(s, axis=0, keepdims=True))
                alpha = jnp.exp2(m - m_new)
                p = jnp.exp2(s - m_new)
                acc_ref[c] = alpha * acc_ref[c] + _dot(vt, p.astype(BF16))
                out += [m_new, alpha * l + jnp.sum(p, axis=0, keepdims=True)]
            return tuple(out)

        init_m = jnp.full((1, tq), NEG_INF, F32)
        init_l = jnp.zeros((1, tq), F32)
        carry = lax.fori_loop(0, qi * (tq // tg), lambda j, cr: tile(j * tg, cr),
                              (init_m, init_l, init_m, init_l))
        for j in range(tq // tg):
            carry = tile(diag + j * tg, carry, key_chunk_offset=j * (tg // CHUNK))
        l_ref[0, 0:1, :] = carry[1]
        l_ref[1, 0:1, :] = carry[3]
        for g in range(n_groups):
            finalize(g)


def _diff_attn(bound, q, k, vt, lam_vecs, gsub, *, layer, tq, lam_init):
    b, nh, s, dv = q.shape
    assert tq % (4 * KV_TILE) == 0 and tq % GENERAL_KV_TILE == 0 and s % tq == 0
    assert QUERY_GROUP == KV_TILE
    p_buffer = pltpu.VMEM((2, KV_TILE, tq), BF16)
    return pl.pallas_call(
        functools.partial(_diff_attn_kernel, layer=layer, tq=tq, lam_init=lam_init),
        grid=(b, nh, s // tq),
        in_specs=[pl.BlockSpec(memory_space=pltpu.SMEM),
                  pl.BlockSpec((None, None, tq, dv), lambda bi, hi, qi: (bi, hi, qi, 0)),
                  pl.BlockSpec((None, None, s, dv), lambda bi, hi, qi: (bi, hi, 0, 0)),
                  pl.BlockSpec((None, dv, s), lambda bi, hi, qi: (bi, hi, 0)),
                  _layer_resident(lam_vecs.shape, layer), _layer_resident(gsub.shape, layer)],
        out_specs=pl.BlockSpec((None, tq, dv), lambda bi, hi, qi: (bi, qi, hi)),
        out_shape=jax.ShapeDtypeStruct((b, s, nh * dv), BF16),
        scratch_shapes=[p_buffer, p_buffer, p_buffer, p_buffer,
                        pltpu.VMEM((2, dv, tq), F32), pltpu.VMEM((2, 8, tq), F32)],
        compiler_params=pltpu.CompilerParams(
            dimension_semantics=("arbitrary", "arbitrary", "arbitrary"),
            vmem_limit_bytes=VMEM_LIMIT_BYTES),
        name="diff_attn",
    )(bound, q, k, vt, lam_vecs, gsub)


def _mixer_out_kernel(h_ref, partial_ref, gc_ref, yc_ref, p_ref, w_attn_out_ref, w_o_ref,
                      gmlp_ref, w_up_ref, w_down_ref, gple_ref, w_gate_ref, w_proj_ref,
                      out_ref, *, ff_chunk):
    merged = partial_ref[...] + gc_ref[...] * _dot(yc_ref[...], w_attn_out_ref[...])
    h1 = h_ref[...] + _dot(merged.astype(BF16), w_o_ref[...])

    m = _rms_rows(h1, gmlp_ref[...]).astype(BF16)
    d_ff = w_up_ref.shape[1]
    mlp = jnp.zeros_like(h1)
    for lo in range(0, d_ff, ff_chunk):
        up = _dot(m, w_up_ref[:, lo:lo + ff_chunk])
        act = jnp.square(jnp.maximum(up, 0.0)).astype(BF16)
        mlp = mlp + _dot(act, w_down_ref[lo:lo + ff_chunk, :])
    h2 = h1 + mlp
    embed = _dot(p_ref[...].astype(BF16), w_proj_ref[...])

    e = _rms_rows(h2, gple_ref[...]).astype(BF16)
    gate = jax.nn.sigmoid(_dot(e, w_gate_ref[...]))
    out_ref[...] = h2 + gate * embed


def _mixer_out(h, partial, gc, yc, p, w_attn_out, w_o, gmlp, w_up, w_down, gple, w_gate, w_proj,
               *, layer, tm, ff_chunk):
    b, s, d = h.shape
    lr = lambda a: _layer_resident(a.shape, layer)
    tok = lambda width: pl.BlockSpec((None, tm, width), lambda bi, si: (bi, si, 0))
    p_tok = pl.BlockSpec((None, None, tm, p.shape[-1]), lambda bi, si: (layer, bi, si, 0))
    return pl.pallas_call(
        functools.partial(_mixer_out_kernel, ff_chunk=ff_chunk),
        grid=(b, s // tm),
        in_specs=[tok(d), tok(d), tok(d), tok(yc.shape[-1]), p_tok,
                  lr(w_attn_out), lr(w_o), lr(gmlp), lr(w_up), lr(w_down), lr(gple), lr(w_gate),
                  lr(w_proj)],
        out_specs=tok(d),
        out_shape=jax.ShapeDtypeStruct((b, s, d), F32),
        compiler_params=pltpu.CompilerParams(
            dimension_semantics=("arbitrary", "arbitrary"),
            vmem_limit_bytes=VMEM_LIMIT_BYTES),
        name="mixer_out",
    )(h, partial, gc, yc, p, w_attn_out, w_o, gmlp, w_up, w_down, gple, w_gate, w_proj)


def _block_diag(blocks):
    g, n, _ = blocks.shape
    eye = jnp.eye(g, dtype=blocks.dtype)
    return (eye[:, None, :, None] * blocks[:, :, None, :]).reshape(g * n, g * n)


def kernel(x, p, norm_mix_g, w_in, pool_w, pool_scale, conv_w, q_norm_g, k_norm_g, lam_q1, lam_k1,
           lam_q2, lam_k2, sub_norm_g, w_pool_out, w_conv_out, w_attn_out, w_o, norm_mlp_g, w_up,
           w_down, norm_ple_g, w_ple_gate, w_ple_proj):
    depth = w_in.shape[0]
    c = MIX_WIDTH
    tm_in, tm_out, tq, ff_chunk = 512, 512, 2048, 1024
    n_seg = c // ATTN_HEAD_DIM
    seg = _block_diag(jnp.ones((n_seg, ATTN_HEAD_DIM, ATTN_HEAD_DIM), BF16))
    rows = lambda v: v.reshape(depth, 1, -1).astype(F32)
    bf = lambda w: w.astype(BF16)
    score_scale = ATTN_HEAD_DIM ** -0.5 * math.log2(math.e)

    gq = rows(jnp.tile(q_norm_g, (1, n_seg)) * score_scale)
    gk = rows(jnp.tile(k_norm_g, (1, n_seg)))
    pool_bd = bf(jax.vmap(_block_diag)(pool_w))
    lam_vecs = jnp.stack([lam_q1, lam_k1, lam_q2, lam_k2], axis=1).astype(F32)
    score_bound = (64.0 * SCORE_BOUND_MARGIN * score_scale * jnp.max(jnp.abs(q_norm_g), axis=-1)
                   * jnp.max(jnp.abs(k_norm_g), axis=-1)).astype(F32)
    mixer_in_params = (rows(norm_mix_g), bf(w_in), seg, gq, gk, pool_bd, rows(pool_scale),
                       conv_w.astype(F32), bf(w_pool_out), bf(w_conv_out))
    mixer_out_params = (bf(w_attn_out), bf(w_o), rows(norm_mlp_g), bf(w_up), bf(w_down),
                        rows(norm_ple_g), bf(w_ple_gate), bf(w_ple_proj))
    gsub = rows(sub_norm_g)

    h = x
    for i in range(depth):
        lam_init = 0.8 - 0.6 * math.exp(-0.3 * i)
        partial, gc, q, k, vt = _mixer_in(h, *mixer_in_params, layer=i, tm=tm_in)
        yc = _diff_attn(score_bound, q, k, vt, lam_vecs, gsub, layer=i, tq=tq, lam_init=lam_init)
        h = _mixer_out(h, partial, gc, yc, p, *mixer_out_params, layer=i, tm=tm_out,
                       ff_chunk=ff_chunk)
    return h
```

```python
import functools
import math

import jax
import jax.numpy as jnp
from jax import lax
from jax.experimental import pallas as pl
from jax.experimental.pallas import tpu as pltpu

MIX_WIDTH = 512
POOL_WINDOWS = (2, 4, 8, 16)
POOL_GROUP_DIM = MIX_WIDTH // len(POOL_WINDOWS)
CONV_WIDTH = 3
ATTN_HEADS = 4
ATTN_HEAD_DIM = 64
ATTN_V_DIM = 2 * ATTN_HEAD_DIM
CHUNK = 64
MXU_TILE = 256
QUERY_GROUP = MXU_TILE
KV_TILE = 256
GENERAL_KV_TILE = 256
MAX_STATIC_SHIFT = 30.0
SCORE_BOUND_MARGIN = 1.02
EPS = 1e-6
NEG_INF = -1e30

POOL_HALO = 16
CONV_HALO = 8
VMEM_LIMIT_BYTES = 56 * 1024 * 1024

F32 = jnp.float32
BF16 = jnp.bfloat16


def _dot(a, b):
    return jnp.dot(a, b, preferred_element_type=F32)


def _dot_nt(a, b):
    return lax.dot_general(a, b, (((1,), (1,)), ((), ())), preferred_element_type=F32)


def _block_diag_dot(x, w_ref):
    n = w_ref.shape[0]
    tiles = [slice(lo, lo + MXU_TILE) for lo in range(0, n, MXU_TILE)]
    return jnp.concatenate([_dot(x[:, t], w_ref[t, t]) for t in tiles], axis=-1)


def _rms_rows(x, g):
    ms = jnp.mean(x * x, axis=-1, keepdims=True)
    return x * lax.rsqrt(ms + EPS) * g


def _resident(shape):
    nd = len(shape)
    return pl.BlockSpec(shape, lambda *_: (0,) * nd, pipeline_mode=pl.Buffered(1))


def _layer_resident(stacked_shape, layer):
    nd = len(stacked_shape)
    return pl.BlockSpec((None,) + tuple(stacked_shape[1:]), lambda *_: (layer,) + (0,) * (nd - 1),
                        pipeline_mode=pl.Buffered(1))


def _mixer_in_kernel(h_ref, gmix_ref, w_in_ref, seg_ref, gq_ref, gk_ref,
                     pool_w_ref, pool_scale_ref, conv_w_ref, w_pool_out_ref, w_conv_out_ref,
                     partial_ref, gc_ref, q_ref, k_ref, vt_ref,
                     abuf_ref, zbuf_ref, vbuf_ref, *, tm):
    s_idx = pl.program_id(1)
    c = MIX_WIDTH

    @pl.when(s_idx == 0)
    def _():
        abuf_ref[0:POOL_HALO, :] = jnp.zeros((POOL_HALO, c), F32)
        zbuf_ref[0:CONV_HALO, :] = jnp.zeros((CONV_HALO, c), F32)

    u = _rms_rows(h_ref[...], gmix_ref[...]).astype(BF16)

    def proj(lo, width):
        return _dot(u, w_in_ref[:, lo:lo + width])

    gate_lo = 7 * c
    d = partial_ref.shape[-1]
    a = proj(0, c)
    abuf_ref[POOL_HALO:POOL_HALO + tm, :] = a
    z = proj(3 * c, c) * proj(c, c)
    zbuf_ref[CONV_HALO:CONV_HALO + tm, :] = z
    gc_ref[...] = jax.nn.sigmoid(proj(gate_lo + 2 * d, d))

    t_pos = (s_idx * tm + lax.broadcasted_iota(jnp.int32, (tm, POOL_GROUP_DIM), 0) + 1).astype(F32)
    pooled = []
    for g, win in enumerate(POOL_WINDOWS):
        lo = g * POOL_GROUP_DIM
        wsum = a[:, lo:lo + POOL_GROUP_DIM]
        for back in range(1, win):
            wsum = wsum + abuf_ref[POOL_HALO - back:POOL_HALO - back + tm, lo:lo + POOL_GROUP_DIM]
        cnt = jnp.minimum(t_pos, float(win))
        pooled.append(wsum / cnt - a[:, lo:lo + POOL_GROUP_DIM])
    pooled = jnp.concatenate(pooled, axis=-1).astype(BF16)
    y_a = _block_diag_dot(pooled, pool_w_ref) * pool_scale_ref[...]
    abuf_ref[0:POOL_HALO, :] = abuf_ref[tm:tm + POOL_HALO, :]

    vbuf_ref[...] = proj(6 * c, c)
    vt_ref[...] = vbuf_ref[...].T.astype(BF16)

    conv = conv_w_ref[CONV_WIDTH - 1:CONV_WIDTH, :] * z
    for back in range(1, CONV_WIDTH):
        tap = conv_w_ref[CONV_WIDTH - 1 - back:CONV_WIDTH - back, :]
        conv = conv + tap * zbuf_ref[CONV_HALO - back:CONV_HALO - back + tm, :]
    y_b = proj(2 * c, c) * conv
    zbuf_ref[0:CONV_HALO, :] = zbuf_ref[tm:tm + CONV_HALO, :]

    g_a = jax.nn.sigmoid(proj(gate_lo, d))
    part = g_a * _dot(y_a.astype(BF16), w_pool_out_ref[...])
    g_b = jax.nn.sigmoid(proj(gate_lo + d, d))
    part = part + g_b * _dot(y_b.astype(BF16), w_conv_out_ref[...])
    partial_ref[...] = part

    def qk_norm(x, g):
        ss = _block_diag_dot((x * x).astype(BF16), seg_ref)
        return (x * lax.rsqrt(ss * (1.0 / ATTN_HEAD_DIM) + EPS) * g).astype(BF16)

    qn = qk_norm(proj(4 * c, c), gq_ref[...])
    kn = qk_norm(proj(5 * c, c), gk_ref[...])
    for hd in range(ATTN_HEADS):
        q_ref[hd] = qn[:, hd * ATTN_V_DIM:(hd + 1) * ATTN_V_DIM]
        k_ref[hd] = kn[:, hd * ATTN_V_DIM:(hd + 1) * ATTN_V_DIM]


def _mixer_in(h, gmix, w_in, seg, gq, gk, pool_w, pool_scale, conv_w, w_pool_out,
              w_conv_out, *, layer, tm):
    b, s, d = h.shape
    lr = lambda a: _layer_resident(a.shape, layer)
    c = MIX_WIDTH
    tok = lambda width: pl.BlockSpec((None, tm, width), lambda bi, si: (bi, si, 0))
    head_tok = pl.BlockSpec((None, ATTN_HEADS, tm, ATTN_V_DIM), lambda bi, si: (bi, 0, si, 0))
    return pl.pallas_call(
        functools.partial(_mixer_in_kernel, tm=tm),
        grid=(b, s // tm),
        in_specs=[tok(d), lr(gmix), lr(w_in), _resident(seg.shape), lr(gq), lr(gk),
                  lr(pool_w), lr(pool_scale), lr(conv_w), lr(w_pool_out), lr(w_conv_out)],
        out_specs=[tok(d), tok(d), head_tok, head_tok,
                   pl.BlockSpec((None, c, tm), lambda bi, si: (bi, 0, si))],
        out_shape=[jax.ShapeDtypeStruct((b, s, d), F32),
                   jax.ShapeDtypeStruct((b, s, d), F32),
                   jax.ShapeDtypeStruct((b, ATTN_HEADS, s, ATTN_V_DIM), BF16),
                   jax.ShapeDtypeStruct((b, ATTN_HEADS, s, ATTN_V_DIM), BF16),
                   jax.ShapeDtypeStruct((b, c, s), BF16)],
        scratch_shapes=[pltpu.VMEM((POOL_HALO + tm, c), F32),
                        pltpu.VMEM((CONV_HALO + tm, c), F32),
                        pltpu.VMEM((tm, c), F32)],
        compiler_params=pltpu.CompilerParams(
            dimension_semantics=("arbitrary", "arbitrary"),
            vmem_limit_bytes=VMEM_LIMIT_BYTES,
            allow_input_fusion=[i in (2, 9, 10) for i in range(11)]),
        name="mixer_in",
    )(h, gmix, w_in, seg, gq, gk, pool_w, pool_scale, conv_w, w_pool_out, w_conv_out)


def _diff_attn_kernel(bound_ref, q_ref, k_ref, vt_ref, lam_ref, gsub_ref, o_ref,
                      pa0_ref, pa1_ref, pb0_ref, pb1_ref, acc_ref, l_ref, *, layer, tq, lam_init):
    qi = pl.program_id(2)
    q = q_ref[...]
    lane = lax.broadcasted_iota(jnp.int32, q.shape, 1)
    zero = jnp.zeros_like(q)
    q_halves = (jnp.where(lane < ATTN_HEAD_DIM, q, zero), jnp.where(lane >= ATTN_HEAD_DIM, q, zero))
    bound = bound_ref[layer]
    diag = qi * tq
    n_groups = tq // QUERY_GROUP
    acc_ref[...] = jnp.zeros_like(acc_ref)
    l_ref[...] = jnp.zeros_like(l_ref)

    def chunk_mask(shape, key_chunk_offset):
        key_chunk = lax.broadcasted_iota(jnp.int32, shape, 0) // CHUNK + key_chunk_offset
        q_chunk = lax.broadcasted_iota(jnp.int32, shape, 1) // CHUNK
        return key_chunk <= q_chunk

    lam = (jnp.exp(jnp.sum(lam_ref[0:1, :] * lam_ref[1:2, :], axis=-1, keepdims=True))
           - jnp.exp(jnp.sum(lam_ref[2:3, :] * lam_ref[3:4, :], axis=-1, keepdims=True))
           + lam_init)

    def finalize(g):
        cols = slice(g * QUERY_GROUP, (g + 1) * QUERY_GROUP)
        l1 = jnp.sum(l_ref[0, :, cols], axis=0, keepdims=True)
        l2 = jnp.sum(l_ref[1, :, cols], axis=0, keepdims=True)
        o = acc_ref[0, :, cols] / l1 - lam * (acc_ref[1, :, cols] / l2)
        ms = jnp.mean(o * o, axis=0, keepdims=True)
        o = o * lax.rsqrt(ms + EPS)
        o_ref[cols, :] = (o.T * gsub_ref[...] * (1.0 - lam_init)).astype(o_ref.dtype)

    @pl.when(bound <= MAX_STATIC_SHIFT)
    def _():
        tk = KV_TILE
        set_a, set_b = (pa0_ref, pa1_ref), (pb0_ref, pb1_ref)
        pieces = [(c, g) for c in range(2) for g in range(n_groups)]

        def stage(probs=None, acc=None):
            if probs is not None:
                k = k_ref[pl.ds(pl.multiple_of(probs[0], tk), tk), :]
            if acc is not None:
                vt = vt_ref[:, pl.ds(pl.multiple_of(acc[0], tk), tk)]
            for c, g in pieces:
                cols = slice(g * QUERY_GROUP, (g + 1) * QUERY_GROUP)
                if probs is not None and (probs[2] is None or g >= probs[2]):
                    s = _dot_nt(k, q_halves[c][cols, :])
                    p = jnp.exp2(s - bound)
                    if probs[2] == g:
                        p = jnp.where(chunk_mask(s.shape, 0), p, 0.0)
                    l_ref[c, :, cols] += jnp.sum(p.reshape(tk // 8, 8, QUERY_GROUP), axis=0)
                    probs[1][c, :, cols] = p.astype(BF16)
                if acc is not None and (acc[2] is None or g >= acc[2]):
                    acc_ref[c, :, cols] += _dot(vt, acc[1][c, :, cols])

        stage(probs=(diag, set_b[0], 0))
        stage(probs=(diag + tk, set_b[1], 1))
        for c in range(2):
            set_b[1][c, :, 0:QUERY_GROUP] = jnp.zeros((tk, QUERY_GROUP), BF16)

        def quad(r, carry):
            base = 4 * r * tk
            first = r == 0
            stage(probs=(base, set_a[0], None),
                  acc=(jnp.where(first, diag, base - 2 * tk), set_b[0], None))
            stage(probs=(base + tk, set_a[1], None),
                  acc=(jnp.where(first, diag + tk, base - tk), set_b[1], None))
            stage(probs=(base + 2 * tk, set_b[0], None), acc=(base, set_a[0], None))
            stage(probs=(base + 3 * tk, set_b[1], None), acc=(base + tk, set_a[1], None))
            return carry

        n_diag = tq // tk
        lax.fori_loop(0, qi * n_diag // 4, quad, 0)
        none_below = qi == 0
        pending = [(jnp.where(none_below, diag, diag - 2 * tk), set_b[0], None),
                   (jnp.where(none_below, diag + tk, diag - tk), set_b[1], None)]
        free, busy = set_a, set_b
        for j in range(2, n_diag, 2):
            filled = [(diag + (j + i) * tk, free[i], j + i) for i in range(2)]
            for i in range(2):
                stage(probs=filled[i], acc=pending[i])
            pending, free, busy = filled, busy, free
            finalize(j - 2)
            finalize(j - 1)
        for i in range(2):
            stage(acc=pending[i])
        finalize(n_diag - 2)
        finalize(n_diag - 1)

    @pl.when(bound > MAX_STATIC_SHIFT)
    def _():
        tg = GENERAL_KV_TILE

        def tile(start, carry, key_chunk_offset=None):
            start = pl.multiple_of(start, tg)
            k = k_ref[pl.ds(start, tg), :]
            vt = vt_ref[:, pl.ds(start, tg)]
            out = []
            for c in range(2):
                m, l = carry[2 * c], carry[2 * c + 1]
                s = _dot_nt(k, q_halves[c])
                if key_chunk_offset is not None:
                    s = jnp.where(chunk_mask(s.shape, key_chunk_offset), s, NEG_INF)
                m_new = jnp.maximum(m, jnp.max(s, axis=0, keepdims=True))
                alpha = jnp.exp2(m - m_new)
                p = jnp.exp2(s - m_new)
                acc_ref[c] = alpha * acc_ref[c] + _dot(vt, p.astype(BF16))
                out += [m_new, alpha * l + jnp.sum(p, axis=0, keepdims=True)]
            return tuple(out)

        init_m = jnp.full((1, tq), NEG_INF, F32)
        init_l = jnp.zeros((1, tq), F32)
        carry = lax.fori_loop(0, qi * (tq // tg), lambda j, cr: tile(j * tg, cr),
                              (init_m, init_l, init_m, init_l))
        for j in range(tq // tg):
            carry = tile(diag + j * tg, carry, key_chunk_offset=j * (tg // CHUNK))
        l_ref[0, 0:1, :] = carry[1]
        l_ref[1, 0:1, :] = carry[3]
        for g in range(n_groups):
            finalize(g)


def _diff_attn(bound, q, k, vt, lam_vecs, gsub, *, layer, tq, lam_init):
    b, nh, s, dv = q.shape
    assert tq % (4 * KV_TILE) == 0 and tq % GENERAL_KV_TILE == 0 and s % tq == 0
    assert QUERY_GROUP == KV_TILE
    p_buffer = pltpu.VMEM((2, KV_TILE, tq), BF16)
    return pl.pallas_call(
        functools.partial(_diff_attn_kernel, layer=layer, tq=tq, lam_init=lam_init),
        grid=(b, nh, s // tq),
        in_specs=[pl.BlockSpec(memory_space=pltpu.SMEM),
                  pl.BlockSpec((None, None, tq, dv), lambda bi, hi, qi: (bi, hi, qi, 0)),
                  pl.BlockSpec((None, None, s, dv), lambda bi, hi, qi: (bi, hi, 0, 0)),
                  pl.BlockSpec((None, dv, s), lambda bi, hi, qi: (bi, hi, 0)),
                  _layer_resident(lam_vecs.shape, layer), _layer_resident(gsub.shape, layer)],
        out_specs=pl.BlockSpec((None, tq, dv), lambda bi, hi, qi: (bi, qi, hi)),
        out_shape=jax.ShapeDtypeStruct((b, s, nh * dv), BF16),
        scratch_shapes=[p_buffer, p_buffer, p_buffer, p_buffer,
                        pltpu.VMEM((2, dv, tq), F32), pltpu.VMEM((2, 8, tq), F32)],
        compiler_params=pltpu.CompilerParams(
            dimension_semantics=("arbitrary", "arbitrary", "arbitrary"),
            vmem_limit_bytes=VMEM_LIMIT_BYTES),
        name="diff_attn",
    )(bound, q, k, vt, lam_vecs, gsub)


def _mixer_out_kernel(h_ref, partial_ref, gc_ref, yc_ref, p_ref, w_attn_out_ref, w_o_ref,
                      gmlp_ref, w_up_ref, w_down_ref, gple_ref, w_gate_ref, w_proj_ref,
                      out_ref, *, ff_chunk):
    merged = partial_ref[...] + gc_ref[...] * _dot(yc_ref[...], w_attn_out_ref[...])
    h1 = h_ref[...] + _dot(merged.astype(BF16), w_o_ref[...])

    m = _rms_rows(h1, gmlp_ref[...]).astype(BF16)
    d_ff = w_up_ref.shape[1]
    mlp = jnp.zeros_like(h1)
    for lo in range(0, d_ff, ff_chunk):
        up = _dot(m, w_up_ref[:, lo:lo + ff_chunk])
        act = jnp.square(jnp.maximum(up, 0.0)).astype(BF16)
        mlp = mlp + _dot(act, w_down_ref[lo:lo + ff_chunk, :])
    h2 = h1 + mlp
    embed = _dot(p_ref[...].astype(BF16), w_proj_ref[...])

    e = _rms_rows(h2, gple_ref[...]).astype(BF16)
    gate = jax.nn.sigmoid(_dot(e, w_gate_ref[...]))
    out_ref[...] = h2 + gate * embed


def _mixer_out(h, partial, gc, yc, p, w_attn_out, w_o, gmlp, w_up, w_down, gple, w_gate, w_proj,
               *, layer, tm, ff_chunk):
    b, s, d = h.shape
    lr = lambda a: _layer_resident(a.shape, layer)
    tok = lambda width: pl.BlockSpec((None, tm, width), lambda bi, si: (bi, si, 0))
    p_tok = pl.BlockSpec((None, None, tm, p.shape[-1]), lambda bi, si: (layer, bi, si, 0))
    return pl.pallas_call(
        functools.partial(_mixer_out_kernel, ff_chunk=ff_chunk),
        grid=(b, s // tm),
        in_specs=[tok(d), tok(d), tok(d), tok(yc.shape[-1]), p_tok,
                  lr(w_attn_out), lr(w_o), lr(gmlp), lr(w_up), lr(w_down), lr(gple), lr(w_gate),
                  lr(w_proj)],
        out_specs=tok(d),
        out_shape=jax.ShapeDtypeStruct((b, s, d), F32),
        compiler_params=pltpu.CompilerParams(
            dimension_semantics=("arbitrary", "arbitrary"),
            vmem_limit_bytes=VMEM_LIMIT_BYTES,
            allow_input_fusion=[i in (5, 6, 8, 9, 11, 12) for i in range(13)]),
        name="mixer_out",
    )(h, partial, gc, yc, p, w_attn_out, w_o, gmlp, w_up, w_down, gple, w_gate, w_proj)


def _block_diag(blocks):
    g, n, _ = blocks.shape
    eye = jnp.eye(g, dtype=blocks.dtype)
    return (eye[:, None, :, None] * blocks[:, :, None, :]).reshape(g * n, g * n)


def kernel(x, p, norm_mix_g, w_in, pool_w, pool_scale, conv_w, q_norm_g, k_norm_g, lam_q1, lam_k1,
           lam_q2, lam_k2, sub_norm_g, w_pool_out, w_conv_out, w_attn_out, w_o, norm_mlp_g, w_up,
           w_down, norm_ple_g, w_ple_gate, w_ple_proj):
    depth = w_in.shape[0]
    c = MIX_WIDTH
    tm_in, tm_out, tq, ff_chunk = 512, 512, 2048, 1024
    n_seg = c // ATTN_HEAD_DIM
    seg = _block_diag(jnp.ones((n_seg, ATTN_HEAD_DIM, ATTN_HEAD_DIM), BF16))
    rows = lambda v: v.reshape(depth, 1, -1).astype(F32)
    bf = lambda w: w.astype(BF16)
    score_scale = ATTN_HEAD_DIM ** -0.5 * math.log2(math.e)

    gq = rows(jnp.tile(q_norm_g, (1, n_seg)) * score_scale)
    gk = rows(jnp.tile(k_norm_g, (1, n_seg)))
    pool_bd = bf(jax.vmap(_block_diag)(pool_w))
    lam_vecs = jnp.stack([lam_q1, lam_k1, lam_q2, lam_k2], axis=1).astype(F32)
    score_bound = (64.0 * SCORE_BOUND_MARGIN * score_scale * jnp.max(jnp.abs(q_norm_g), axis=-1)
                   * jnp.max(jnp.abs(k_norm_g), axis=-1)).astype(F32)
    mixer_in_params = (rows(norm_mix_g), bf(w_in), seg, gq, gk, pool_bd, rows(pool_scale),
                       conv_w.astype(F32), bf(w_pool_out), bf(w_conv_out))
    mixer_out_params = (bf(w_attn_out), bf(w_o), rows(norm_mlp_g), bf(w_up), bf(w_down),
                        rows(norm_ple_g), bf(w_ple_gate), bf(w_ple_proj))
    gsub = rows(sub_norm_g)

    h = x
    for i in range(depth):
        lam_init = 0.8 - 0.6 * math.exp(-0.3 * i)
        partial, gc, q, k, vt = _mixer_in(h, *mixer_in_params, layer=i, tm=tm_in)
        yc = _diff_attn(score_bound, q, k, vt, lam_vecs, gsub, layer=i, tq=tq, lam_init=lam_init)
        h = _mixer_out(h, partial, gc, yc, p, *mixer_out_params, layer=i, tm=tm_out,
                       ff_chunk=ff_chunk)
    return h
```
